```python
import jax, jax.numpy as jnp
from jax import lax
import numpy as np

D_MODEL = 2048
BATCH = 2
SEQ = 8192
DEPTH = 2
DEC_BATCH = 1
DEC_SEQ = 16384
PAST_LEN = 128

N_MEM = 256
D_MIX = D_MODEL
HEAD_DIM = 64
ATTN_WIDTH = D_MIX // 2
N_Q_HEADS = ATTN_WIDTH // HEAD_DIM
N_KV_HEADS = 4
KV_GROUP = N_Q_HEADS // N_KV_HEADS
KV_WIDTH = N_KV_HEADS * HEAD_DIM
WINDOW = 128
BLOCK = 128
CONV_WIDTH = D_MIX // 4
CONV_K = 3
N_X_HEADS = 4
X_WIDTH = D_MIX // 4
X_HEAD_DIM = X_WIDTH // N_X_HEADS
ROPE_THETA = 10000.0
EPS = 1e-6
IN_SIZES = (ATTN_WIDTH, KV_WIDTH, KV_WIDTH, ATTN_WIDTH,
            CONV_WIDTH, CONV_WIDTH, CONV_WIDTH, CONV_WIDTH,
            X_WIDTH, X_WIDTH)
D_IN = sum(IN_SIZES)

kernel_name = "hymba_style_window_gqa_shortconv_memxattn_encoder"


def rmsnorm(x, g):
    x32 = x.astype(jnp.float32)
    y = x32 * lax.rsqrt(jnp.mean(x32 * x32, axis=-1, keepdims=True) + EPS)
    return y.astype(x.dtype) * g


def split_cols(p):
    outs, start = [], 0
    for size in IN_SIZES:
        outs.append(p[..., start:start + size])
        start += size
    return outs


def rope(x):
    s, d = x.shape[1], x.shape[-1]
    inv_freq = ROPE_THETA ** (-jnp.arange(0, d, 2, dtype=jnp.float32) / d)
    ang = jnp.arange(s, dtype=jnp.float32)[:, None] * inv_freq[None, :]
    cos = jnp.cos(ang)[None, :, None, :].astype(x.dtype)
    sin = jnp.sin(ang)[None, :, None, :].astype(x.dtype)
    x1, x2 = x[..., : d // 2], x[..., d // 2:]
    return jnp.concatenate([x1 * cos - x2 * sin, x2 * cos + x1 * sin], axis=-1)


def window_attention(q, k, v, sink):
    b, s, _, d = q.shape
    nb = s // BLOCK
    qb = q.reshape(b, nb, BLOCK, N_KV_HEADS, KV_GROUP, d)
    pad = ((0, 0), (BLOCK, BLOCK), (0, 0), (0, 0))
    kp = jnp.pad(k, pad).reshape(b, nb + 2, BLOCK, N_KV_HEADS, d)
    vp = jnp.pad(v, pad).reshape(b, nb + 2, BLOCK, N_KV_HEADS, d)
    kb = jnp.concatenate([kp[:, :-2], kp[:, 1:-1], kp[:, 2:]], axis=2)
    vb = jnp.concatenate([vp[:, :-2], vp[:, 1:-1], vp[:, 2:]], axis=2)
    scores = jnp.einsum('bnqhgd,bnkhd->bnhgqk', qb, kb).astype(jnp.float32) * (d ** -0.5)
    qpos = jnp.arange(s).reshape(nb, BLOCK)
    kpos = qpos[:, :1] - BLOCK + jnp.arange(3 * BLOCK)[None, :]
    rel = kpos[:, None, :] - qpos[:, :, None]
    valid = (jnp.abs(rel) <= WINDOW) & (kpos[:, None, :] >= 0) & (kpos[:, None, :] < s)
    scores = jnp.where(valid[None, :, None, None], scores, -jnp.inf)
    sink_logit = sink.astype(jnp.float32).reshape(N_KV_HEADS, KV_GROUP)[None, None, :, :, None, None]
    sink_logit = jnp.broadcast_to(sink_logit, scores.shape[:-1] + (1,))
    probs = jax.nn.softmax(jnp.concatenate([scores, sink_logit], axis=-1), axis=-1)[..., :-1]
    out = jnp.einsum('bnhgqk,bnkhd->bnqhgd', probs.astype(v.dtype), vb)
    return out.reshape(b, s, N_Q_HEADS * d)


def memory_attention(q, mk, mv):
    b, s = q.shape[0], q.shape[1]
    scores = jnp.einsum('bshd,bmhd->bhsm', q, mk).astype(jnp.float32) * (X_HEAD_DIM ** -0.5)
    probs = jax.nn.softmax(scores, axis=-1)
    out = jnp.einsum('bhsm,bmhd->bshd', probs.astype(mv.dtype), mv)
    return out.reshape(b, s, X_WIDTH)


def short_conv(z, conv_w):
    s = z.shape[1]
    half = CONV_K // 2
    zp = jnp.pad(z, ((0, 0), (half, half), (0, 0)))
    return sum(conv_w[t] * zp[:, t:t + s] for t in range(CONV_K))


def encoder_layer(x, mem, norm_in, w_in, sink, conv_w, norm_mem, w_mem_kv,
                  g_attn, g_conv, g_mem, w_out):
    b, s, _ = x.shape
    h = rmsnorm(x, norm_in)
    p = jnp.einsum('bsd,de->bse', h, w_in)
    q, k, v, gate_a, conv_b, conv_c, conv_h, gate_c, mq, gate_m = split_cols(p)
    q = rope(q.reshape(b, s, N_Q_HEADS, HEAD_DIM))
    k = rope(k.reshape(b, s, N_KV_HEADS, HEAD_DIM))
    v = v.reshape(b, s, N_KV_HEADS, HEAD_DIM)
    attn = rmsnorm(window_attention(q, k, v, sink), g_attn) * jax.nn.silu(gate_a)
    conv = conv_b * short_conv(conv_c * conv_h, conv_w)
    conv = rmsnorm(conv, g_conv) * jax.nn.silu(gate_c)
    mkv = jnp.einsum('bmd,de->bme', rmsnorm(mem, norm_mem), w_mem_kv)
    mk = mkv[..., :X_WIDTH].reshape(b, N_MEM, N_X_HEADS, X_HEAD_DIM)
    mv = mkv[..., X_WIDTH:].reshape(b, N_MEM, N_X_HEADS, X_HEAD_DIM)
    xo = memory_attention(mq.reshape(b, s, N_X_HEADS, X_HEAD_DIM), mk, mv)
    xo = rmsnorm(xo, g_mem) * jax.nn.silu(gate_m)
    mixed = jnp.concatenate([attn, conv, xo], axis=-1)
    return x + jnp.einsum('bse,ed->bsd', mixed, w_out)


def trunk(x, mem, norm_in, w_in, attn_sink, conv_w, norm_mem, w_mem_kv,
          g_attn, g_conv, g_mem, w_out, final_norm):
    for l in range(DEPTH):
        x = encoder_layer(x, mem, norm_in[l], w_in[l], attn_sink[l], conv_w[l], norm_mem[l],
                          w_mem_kv[l], g_attn[l], g_conv[l], g_mem[l], w_out[l])
    return rmsnorm(x, final_norm)


def setup_inputs(seed: int = 0) -> dict:
    key = jax.random.key(seed)
    ks = jax.random.split(key, 16)
    f32 = jnp.float32
    nrm = lambda k, shape, scale: jax.random.normal(k, shape, f32) * scale
    return {
        "x_prompt": nrm(ks[0], (BATCH, SEQ, D_MODEL), 1.0),
        "x_sample": nrm(ks[1], (DEC_BATCH, DEC_SEQ, D_MODEL), 1.0),
        "mem_prompt": nrm(ks[2], (BATCH, N_MEM, D_MODEL), 1.0),
        "mem_sample": nrm(ks[3], (DEC_BATCH, N_MEM, D_MODEL), 1.0),
        "norm_in": 1.0 + nrm(ks[4], (DEPTH, D_MODEL), 0.02),
        "w_in": nrm(ks[5], (DEPTH, D_MODEL, D_IN), D_MODEL ** -0.5),
        "attn_sink": nrm(ks[6], (DEPTH, N_Q_HEADS), 0.5),
        "conv_w": nrm(ks[7], (DEPTH, CONV_K, CONV_WIDTH), CONV_K ** -0.5),
        "norm_mem": 1.0 + nrm(ks[8], (DEPTH, D_MODEL), 0.02),
        "w_mem_kv": nrm(ks[9], (DEPTH, D_MODEL, 2 * X_WIDTH), D_MODEL ** -0.5),
        "g_attn": 1.0 + nrm(ks[10], (DEPTH, ATTN_WIDTH), 0.02),
        "g_conv": 1.0 + nrm(ks[11], (DEPTH, CONV_WIDTH), 0.02),
        "g_mem": 1.0 + nrm(ks[12], (DEPTH, X_WIDTH), 0.02),
        "w_out": nrm(ks[13], (DEPTH, D_MIX, D_MODEL), D_MIX ** -0.5),
        "final_norm": 1.0 + nrm(ks[14], (D_MODEL,), 0.02),
    }


def reference(x_prompt, x_sample, mem_prompt, mem_sample, norm_in, w_in, attn_sink, conv_w,
              norm_mem, w_mem_kv, g_attn, g_conv, g_mem, w_out, final_norm):
    y_prompt = trunk(x_prompt, mem_prompt, norm_in, w_in, attn_sink, conv_w, norm_mem, w_mem_kv,
                     g_attn, g_conv, g_mem, w_out, final_norm)
    y_sample = trunk(x_sample, mem_sample, norm_in, w_in, attn_sink, conv_w, norm_mem, w_mem_kv,
                     g_attn, g_conv, g_mem, w_out, final_norm)
    return (y_prompt, y_sample)
```

```python
import functools

import jax
import jax.numpy as jnp
from jax import lax
from jax.experimental import pallas as pl
from jax.experimental.pallas import tpu as pltpu

D_MODEL = 2048
HEAD_DIM = 64
N_Q_HEADS = 16
N_KV_HEADS = 4
ATTN_WIDTH = N_Q_HEADS * HEAD_DIM
KV_WIDTH = N_KV_HEADS * HEAD_DIM
WINDOW = 128
CONV_WIDTH = 512
CONV_K = 3
N_MEM = 256
N_X_HEADS = 4
X_WIDTH = 512
X_HEAD_DIM = X_WIDTH // N_X_HEADS
ROPE_THETA = 10000.0
EPS = 1e-6
D_IN = 5632
CHUNK = 512

LANES = 128
BF16_ROWS = 16
VMEM_LIMIT_BYTES = 56 * 1024 * 1024
NEG = -1e30

ROPE_TILE = 2048
PROJ_TILE = 512
MIX_TILE = 512

BF16 = jnp.bfloat16
F32 = jnp.float32


def _rms(x, g):
    ms = jnp.mean(x * x, axis=-1, keepdims=True)
    return (x * lax.rsqrt(ms + EPS)) * g


def _silu(x):
    return x / (1.0 + jnp.exp(-x))


def _rope_table_kernel(inv_ref, cos_ref, sin_ref):
    i = pl.program_id(0)
    tile = cos_ref.shape[0]
    rows = lax.broadcasted_iota(jnp.int32, (tile, LANES), 0) + i * tile
    lane = lax.broadcasted_iota(jnp.int32, (tile, LANES), 1)
    ang = rows.astype(F32) * inv_ref[...]
    cos_ref[...] = jnp.cos(ang)
    sin_ref[...] = jnp.where((lane % HEAD_DIM) < HEAD_DIM // 2, -jnp.sin(ang), jnp.sin(ang))


def _rope_tables(seq):
    inv_freq = ROPE_THETA ** (-jnp.arange(0, HEAD_DIM, 2, dtype=jnp.float32) / HEAD_DIM)
    inv = jnp.tile(inv_freq, LANES // (HEAD_DIM // 2)).reshape(1, LANES)
    tile = min(ROPE_TILE, seq)
    return pl.pallas_call(
        _rope_table_kernel,
        grid=(seq // tile,),
        in_specs=[pl.BlockSpec((1, LANES), lambda i: (0, 0))],
        out_specs=[pl.BlockSpec((tile, LANES), lambda i: (i, 0))] * 2,
        out_shape=[jax.ShapeDtypeStruct((seq, LANES), F32)] * 2,
        name="rope_tables",
    )(inv)


def _mem_kv_kernel(mem_ref, g_ref, w_ref, o_ref):
    h = _rms(mem_ref[0], g_ref[...]).astype(BF16)
    o_ref[0] = jnp.dot(h, w_ref[...], preferred_element_type=F32).astype(BF16)


def _mem_kv(mem, g, w):
    b = mem.shape[0]
    return pl.pallas_call(
        _mem_kv_kernel,
        grid=(b,),
        in_specs=[pl.BlockSpec((1, N_MEM, D_MODEL), lambda i: (i, 0, 0)),
                  pl.BlockSpec((1, D_MODEL), lambda i: (0, 0)),
                  pl.BlockSpec((D_MODEL, 2 * X_WIDTH), lambda i: (0, 0))],
        out_specs=pl.BlockSpec((1, N_MEM, 2 * X_WIDTH), lambda i: (i, 0, 0)),
        out_shape=jax.ShapeDtypeStruct((b, N_MEM, 2 * X_WIDTH), BF16),
        name="mem_kv",
    )(mem, g.reshape(1, D_MODEL), w)


def _swap_halves(x):
    lane = lax.broadcasted_iota(jnp.int32, x.shape, 1)
    first = (lane % HEAD_DIM) < HEAD_DIM // 2
    return jnp.where(first, pltpu.roll(x, LANES - HEAD_DIM // 2, 1), pltpu.roll(x, HEAD_DIM // 2, 1))


def _in_proj_kernel(x_ref, g_ref, w_ref, cos_ref, sin_ref, q_ref, kv_ref, z_ref, rest_ref):
    h = _rms(x_ref[0], g_ref[...]).astype(BF16)
    cos = cos_ref[...]
    sin = sin_ref[...]

    def proj(c):
        return jnp.dot(h, w_ref[:, c * CHUNK:(c + 1) * CHUNK], preferred_element_type=F32)

    def rope(xc):
        return xc * cos + _swap_halves(xc) * sin

    for c in range(2):
        pc = proj(c)
        for s in range(CHUNK // LANES):
            xc = pc[:, s * LANES:(s + 1) * LANES]
            q_ref[0, :, c * CHUNK + s * LANES:c * CHUNK + (s + 1) * LANES] = (
                rope(xc) * (HEAD_DIM ** -0.5)).astype(BF16)
    pc = proj(2)
    for s in range(CHUNK // LANES):
        xc = pc[:, s * LANES:(s + 1) * LANES]
        if s < KV_WIDTH // LANES:
            xc = rope(xc)
        kv_ref[0, :, s * LANES:(s + 1) * LANES] = xc.astype(BF16)
        kv_ref[0, :, CHUNK + s * LANES:CHUNK + (s + 1) * LANES] = pltpu.roll(xc, HEAD_DIM, 1).astype(BF16)
    for dst, c in enumerate((3, 4, 5, 8, 9, 10)):
        rest_ref[0, :, dst * CHUNK:(dst + 1) * CHUNK] = proj(c).astype(BF16)
    z_ref[0] = (proj(6) * proj(7)).astype(BF16)


def _in_proj(x, g, w, cos, sin):
    b, s, _ = x.shape
    t = PROJ_TILE
    tok = lambda width: pl.BlockSpec((1, t, width), lambda bi, i: (bi, i, 0))
    return pl.pallas_call(
        _in_proj_kernel,
        grid=(b, s // t),
        in_specs=[tok(D_MODEL),
                  pl.BlockSpec((1, D_MODEL), lambda bi, i: (0, 0)),
                  pl.BlockSpec((D_MODEL, D_IN), lambda bi, i: (0, 0), pipeline_mode=pl.Buffered(1)),
                  pl.BlockSpec((t, LANES), lambda bi, i: (i, 0)),
                  pl.BlockSpec((t, LANES), lambda bi, i: (i, 0))],
        out_specs=[tok(ATTN_WIDTH), tok(4 * KV_WIDTH), tok(CONV_WIDTH), tok(6 * CHUNK)],
        out_shape=[jax.ShapeDtypeStruct((b, s, ATTN_WIDTH), BF16),
                   jax.ShapeDtypeStruct((b, s, 4 * KV_WIDTH), BF16),
                   jax.ShapeDtypeStruct((b, s, CONV_WIDTH), BF16),
                   jax.ShapeDtypeStruct((b, s, 6 * CHUNK), BF16)],
        compiler_params=pltpu.CompilerParams(
            dimension_semantics=("parallel", "parallel"), vmem_limit_bytes=VMEM_LIMIT_BYTES),
        name="in_proj",
    )(x, g.reshape(1, D_MODEL), w, cos, sin)


def _mixer_kernel(sink_ref, x_ref, q_ref, kvp_ref, kvc_ref, kvn_ref, zp_ref, zc_ref, zn_ref, rest_ref,
                  mkv_ref, convw_ref, ga_ref, gc_ref, gm_ref, wout_ref, fin_ref, o_ref,
                  attn_ref, mixed_ref, *, final):
    t = MIX_TILE
    nqb = t // WINDOW
    i = pl.program_id(1)
    first = i == 0
    last = i == pl.num_programs(1) - 1

    row = lax.broadcasted_iota(jnp.int32, (WINDOW, WINDOW), 0)
    col = lax.broadcasted_iota(jnp.int32, (WINDOW, WINDOW), 1)
    lo_band = jnp.where(col >= row, 0.0, NEG)
    hi_band = jnp.where(col <= row, 0.0, NEG)
    lane_kv = lax.broadcasted_iota(jnp.int32, (WINDOW, LANES), 1)
    low_half = lane_kv < HEAD_DIM

    def kv_block(blk):
        if blk < 0:
            return kvp_ref[0]
        if blk >= nqb:
            return kvn_ref[0]
        return kvc_ref[0, blk * WINDOW:(blk + 1) * WINDOW, :]

    for j in range(nqb):
        blocks = [kv_block(j - 1), kv_block(j), kv_block(j + 1)]
        lo_bias = lo_band + jnp.where(first, NEG, 0.0) if j == 0 else lo_band
        hi_bias = hi_band + jnp.where(last, NEG, 0.0) if j == nqb - 1 else hi_band
        for g in range(N_KV_HEADS):
            c = g // 2
            k_cols = (c * LANES, 2 * KV_WIDTH + c * LANES)
            v_cols = (KV_WIDTH + c * LANES, 3 * KV_WIDTH + c * LANES)
            if g % 2:
                k_cols, v_cols = k_cols[::-1], v_cols[::-1]

            def diag(cols):
                lo = [jnp.where(low_half, blk[:, cols[0]:cols[0] + LANES], 0) for blk in blocks]
                hi = [jnp.where(low_half, 0, blk[:, cols[1]:cols[1] + LANES]) for blk in blocks]
                return jnp.concatenate(lo + hi, axis=0)

            kd = diag(k_cols)
            vd = diag(v_cols)
            rows = slice(j * WINDOW, (j + 1) * WINDOW)
            qa = q_ref[0, rows, g * 2 * LANES:g * 2 * LANES + LANES]
            qb = q_ref[0, rows, g * 2 * LANES + LANES:(g + 1) * 2 * LANES]
            qq = jnp.concatenate([qa, qb], axis=0)
            s = lax.dot_general(qq, kd, (((1,), (1,)), ((), ())), preferred_element_type=F32)
            probs = []
            inv = []
            for pair in range(2):
                pr = slice(pair * WINDOW, (pair + 1) * WINDOW)
                row_p = []
                row_inv = []
                for half in range(2):
                    off = half * 3 * WINDOW
                    sink = sink_ref[4 * g + 2 * pair + half]
                    s0 = s[pr, off:off + WINDOW] + lo_bias
                    s1 = s[pr, off + WINDOW:off + 2 * WINDOW]
                    s2 = s[pr, off + 2 * WINDOW:off + 3 * WINDOW] + hi_bias
                    m = jnp.max(jnp.maximum(jnp.maximum(s0, s1), s2), axis=-1, keepdims=True)
                    m = jnp.maximum(m, sink)
                    p0 = jnp.exp(s0 - m)
                    p1 = jnp.exp(s1 - m)
                    p2 = jnp.exp(s2 - m)
                    l = jnp.sum(p0 + p1 + p2, axis=-1, keepdims=True) + jnp.exp(sink - m)
                    row_p += [p0.astype(BF16), p1.astype(BF16), p2.astype(BF16)]
                    row_inv.append(1.0 / l)
                probs.append(jnp.concatenate(row_p, axis=1))
                inv.append(jnp.where(low_half, row_inv[0], row_inv[1]))
            pp = jnp.concatenate(probs, axis=0)
            o = jnp.dot(pp, vd, preferred_element_type=F32)
            attn_ref[rows, g * 2 * LANES:g * 2 * LANES + LANES] = o[:WINDOW] * inv[0]
            attn_ref[rows, g * 2 * LANES + LANES:(g + 1) * 2 * LANES] = o[WINDOW:] * inv[1]

    gate_a = rest_ref[0, :, 0:ATTN_WIDTH].astype(F32)
    mixed_ref[:, 0:ATTN_WIDTH] = (_rms(attn_ref[...], ga_ref[...]) * _silu(gate_a)).astype(BF16)

    z = zc_ref[0].astype(F32)
    z_before = jnp.where(first, 0.0, zp_ref[0, BF16_ROWS - 1:BF16_ROWS, :].astype(F32))
    z_after = jnp.where(last, 0.0, zn_ref[0, 0:1, :].astype(F32))
    trow = lax.broadcasted_iota(jnp.int32, z.shape, 0)
    z_prev = jnp.where(trow == 0, z_before, pltpu.roll(z, 1, 0))
    z_next = jnp.where(trow == t - 1, z_after, pltpu.roll(z, t - 1, 0))
    cw = convw_ref[...]
    conv = cw[0:1] * z_prev + cw[1:2] * z + cw[2:3] * z_next
    conv = rest_ref[0, :, 2 * CHUNK:3 * CHUNK].astype(F32) * conv
    gate_c = rest_ref[0, :, 3 * CHUNK:4 * CHUNK].astype(F32)
    mixed_ref[:, ATTN_WIDTH:ATTN_WIDTH + CONV_WIDTH] = (_rms(conv, gc_ref[...]) * _silu(gate_c)).astype(BF16)

    xo = []
    for hd in range(N_X_HEADS):
        mq = rest_ref[0, :, 4 * CHUNK + hd * X_HEAD_DIM:4 * CHUNK + (hd + 1) * X_HEAD_DIM]
        mk = mkv_ref[0, :, hd * X_HEAD_DIM:(hd + 1) * X_HEAD_DIM]
        mv = mkv_ref[0, :, X_WIDTH + hd * X_HEAD_DIM:X_WIDTH + (hd + 1) * X_HEAD_DIM]
        s = lax.dot_general(mq, mk, (((1,), (1,)), ((), ())), preferred_element_type=F32)
        s = s * (X_HEAD_DIM ** -0.5)
        m = jnp.max(s, axis=-1, keepdims=True)
        p = jnp.exp(s - m)
        l = jnp.sum(p, axis=-1, keepdims=True)
        xo.append(jnp.dot(p.astype(BF16), mv, preferred_element_type=F32) * (1.0 / l))
    xo = jnp.concatenate(xo, axis=1)
    gate_m = rest_ref[0, :, 5 * CHUNK:6 * CHUNK].astype(F32)
    mixed_ref[:, ATTN_WIDTH + CONV_WIDTH:] = (_rms(xo, gm_ref[...]) * _silu(gate_m)).astype(BF16)

    y = x_ref[0] + jnp.dot(mixed_ref[...], wout_ref[...], preferred_element_type=F32)
    if final:
        y = _rms(y, fin_ref[...])
    o_ref[0] = y


def _mixer(x, q, kv, z, rest, mkv, sink, conv_w, g_attn, g_conv, g_mem, w_out, final_norm, *, final):
    b, s, _ = x.shape
    t = MIX_TILE
    nblk = t // WINDOW
    nz = t // BF16_ROWS
    tok = lambda width: pl.BlockSpec((1, t, width), lambda bi, i, *_: (bi, i, 0))
    full = lambda r, c: pl.BlockSpec((r, c), lambda bi, i, *_: (0, 0))
    grid_spec = pltpu.PrefetchScalarGridSpec(
        num_scalar_prefetch=1,
        grid=(b, s // t),
        in_specs=[
            tok(D_MODEL),
            tok(ATTN_WIDTH),
            pl.BlockSpec((1, WINDOW, 4 * KV_WIDTH),
                         lambda bi, i, *_: (bi, jnp.maximum(i * nblk - 1, 0), 0)),
            tok(4 * KV_WIDTH),
            pl.BlockSpec((1, WINDOW, 4 * KV_WIDTH),
                         lambda bi, i, *_: (bi, jnp.minimum((i + 1) * nblk, s // WINDOW - 1), 0)),
            pl.BlockSpec((1, BF16_ROWS, CONV_WIDTH),
                         lambda bi, i, *_: (bi, jnp.maximum(i * nz - 1, 0), 0)),
            tok(CONV_WIDTH),
            pl.BlockSpec((1, BF16_ROWS, CONV_WIDTH),
                         lambda bi, i, *_: (bi, jnp.minimum((i + 1) * nz, s // BF16_ROWS - 1), 0)),
            tok(6 * CHUNK),
            pl.BlockSpec((1, N_MEM, 2 * X_WIDTH), lambda bi, i, *_: (bi, 0, 0)),
            full(CONV_K, CONV_WIDTH),
            full(1, ATTN_WIDTH), full(1, CONV_WIDTH), full(1, X_WIDTH),
            pl.BlockSpec((D_MODEL, D_MODEL), lambda bi, i, *_: (0, 0), pipeline_mode=pl.Buffered(1)),
            full(1, D_MODEL),
        ],
        out_specs=tok(D_MODEL),
        scratch_shapes=[pltpu.VMEM((t, ATTN_WIDTH), F32), pltpu.VMEM((t, D_MODEL), BF16)],
    )
    return pl.pallas_call(
        functools.partial(_mixer_kernel, final=final),
        grid_spec=grid_spec,
        out_shape=jax.ShapeDtypeStruct((b, s, D_MODEL), F32),
        compiler_params=pltpu.CompilerParams(
            dimension_semantics=("parallel", "parallel"), vmem_limit_bytes=VMEM_LIMIT_BYTES),
        name="mixer",
    )(sink, x, q, kv, kv, kv, z, z, z, rest, mkv, conv_w,
      g_attn.reshape(1, ATTN_WIDTH), g_conv.reshape(1, CONV_WIDTH), g_mem.reshape(1, X_WIDTH),
      w_out, final_norm.reshape(1, D_MODEL))


def _trunk(x, mem, cos, sin, norm_in, w_in, attn_sink, conv_w, norm_mem, w_mem_kv,
           g_attn, g_conv, g_mem, w_out, final_norm):
    depth = w_in.shape[0]
    s = x.shape[1]
    cos, sin = cos[:s], sin[:s]
    for l in range(depth):
        mkv = _mem_kv(mem, norm_mem[l], w_mem_kv[l])
        q, kv, z, rest = _in_proj(x, norm_in[l], w_in[l], cos, sin)
        x = _mixer(x, q, kv, z, rest, mkv, attn_sink[l], conv_w[l], g_attn[l], g_conv[l], g_mem[l],
                   w_out[l], final_norm, final=(l == depth - 1))
    return x


def kernel(x_prompt, x_sample, mem_prompt, mem_sample, norm_in, w_in, attn_sink, conv_w, norm_mem, w_mem_kv,
           g_attn, g_conv, g_mem, w_out, final_norm):
    cos, sin = _rope_tables(max(x_prompt.shape[1], x_sample.shape[1]))
    weights = (norm_in, w_in.astype(BF16), attn_sink, conv_w, norm_mem, w_mem_kv.astype(BF16),
               g_attn, g_conv, g_mem, w_out.astype(BF16), final_norm)
    y_prompt = _trunk(x_prompt, mem_prompt, cos, sin, *weights)
    y_sample = _trunk(x_sample, mem_sample, cos, sin, *weights)
    return (y_prompt, y_sample)
```

```python
import functools

import jax
import jax.numpy as jnp
from jax import lax
from jax.experimental import pallas as pl
from jax.experimental.pallas import tpu as pltpu

D_MODEL = 2048
HEAD_DIM = 64
N_Q_HEADS = 16
N_KV_HEADS = 4
ATTN_WIDTH = N_Q_HEADS * HEAD_DIM
KV_WIDTH = N_KV_HEADS * HEAD_DIM
WINDOW = 128
CONV_WIDTH = 512
CONV_K = 3
N_MEM = 256
N_X_HEADS = 4
X_WIDTH = 512
X_HEAD_DIM = X_WIDTH // N_X_HEADS
ROPE_THETA = 10000.0
EPS = 1e-6
D_IN = 5632
CHUNK = 512

LANES = 128
BF16_ROWS = 16
VMEM_LIMIT_BYTES = 56 * 1024 * 1024
NEG = -1e30
KVD_WIDTH = N_KV_HEADS * 4 * LANES

ROPE_TILE = 2048
PROJ_TILE = 512
MIX_TILE = 256
OUT_CHUNK = 256
W_OUT_PITCH = D_MODEL + LANES

BF16 = jnp.bfloat16
F32 = jnp.float32


def _rms(x, g):
    ms = jnp.mean(x * x, axis=-1, keepdims=True)
    return (x * lax.rsqrt(ms + EPS)) * g


def _silu(x):
    return x / (1.0 + jnp.exp(-x))


def _rope_table_kernel(inv_ref, cos_ref, sin_ref):
    i = pl.program_id(0)
    tile = cos_ref.shape[0]
    rows = lax.broadcasted_iota(jnp.int32, (tile, LANES), 0) + i * tile
    lane = lax.broadcasted_iota(jnp.int32, (tile, LANES), 1)
    ang = rows.astype(F32) * inv_ref[...]
    cos_ref[...] = jnp.cos(ang)
    sin_ref[...] = jnp.where((lane % HEAD_DIM) < HEAD_DIM // 2, -jnp.sin(ang), jnp.sin(ang))


def _rope_tables(seq):
    inv_freq = ROPE_THETA ** (-jnp.arange(0, HEAD_DIM, 2, dtype=jnp.float32) / HEAD_DIM)
    inv = jnp.tile(inv_freq, LANES // (HEAD_DIM // 2)).reshape(1, LANES)
    tile = min(ROPE_TILE, seq)
    return pl.pallas_call(
        _rope_table_kernel,
        grid=(seq // tile,),
        in_specs=[pl.BlockSpec((1, LANES), lambda i: (0, 0))],
        out_specs=[pl.BlockSpec((tile, LANES), lambda i: (i, 0))] * 2,
        out_shape=[jax.ShapeDtypeStruct((seq, LANES), F32)] * 2,
        name="rope_tables",
    )(inv)


def _mem_kv_kernel(mem_ref, g_ref, w_ref, o_ref):
    h = _rms(mem_ref[0], g_ref[...]).astype(BF16)
    o_ref[0] = jnp.dot(h, w_ref[...], preferred_element_type=F32).astype(BF16)


def _mem_kv(mem, g, w, layer):
    b = mem.shape[0]
    return pl.pallas_call(
        _mem_kv_kernel,
        grid=(b,),
        in_specs=[pl.BlockSpec((1, N_MEM, D_MODEL), lambda i: (i, 0, 0)),
                  pl.BlockSpec((None, 1, D_MODEL), lambda i: (layer, 0, 0)),
                  pl.BlockSpec((None, D_MODEL, 2 * X_WIDTH), lambda i: (layer, 0, 0))],
        out_specs=pl.BlockSpec((1, N_MEM, 2 * X_WIDTH), lambda i: (i, 0, 0)),
        out_shape=jax.ShapeDtypeStruct((b, N_MEM, 2 * X_WIDTH), BF16),
        name="mem_kv",
    )(mem, g, w)


def _swap_halves(x):
    lane = lax.broadcasted_iota(jnp.int32, x.shape, 1)
    first = (lane % HEAD_DIM) < HEAD_DIM // 2
    return jnp.where(first, pltpu.roll(x, LANES - HEAD_DIM // 2, 1), pltpu.roll(x, HEAD_DIM // 2, 1))


def _in_proj_kernel(x_ref, g_ref, w_ref, cos_ref, sin_ref, q_ref, kvd_ref, z_ref, rest_ref):
    h = _rms(x_ref[...], g_ref[...]).astype(BF16)
    cos = cos_ref[...]
    sin = sin_ref[...]
    low_half = lax.broadcasted_iota(jnp.int32, cos.shape, 1) < HEAD_DIM

    def proj(c):
        return jnp.dot(h, w_ref[:, c * CHUNK:(c + 1) * CHUNK], preferred_element_type=F32)

    def rope(xc):
        return xc * cos + _swap_halves(xc) * sin

    for c in range(2):
        pc = proj(c)
        for s in range(CHUNK // LANES):
            xc = pc[:, s * LANES:(s + 1) * LANES]
            q_ref[:, c * CHUNK + s * LANES:c * CHUNK + (s + 1) * LANES] = (
                rope(xc) * (HEAD_DIM ** -0.5)).astype(BF16)
    pc = proj(2)
    for s in range(CHUNK // LANES):
        xc = pc[:, s * LANES:(s + 1) * LANES]
        is_v = s >= KV_WIDTH // LANES
        c = s - KV_WIDTH // LANES if is_v else s
        if not is_v:
            xc = rope(xc)
        sw = pltpu.roll(xc, HEAD_DIM, 1)
        for head, (lo_src, hi_src) in ((2 * c, (xc, sw)), (2 * c + 1, (sw, xc))):
            base = head * 4 * LANES + (2 * LANES if is_v else 0)
            kvd_ref[:, base:base + LANES] = jnp.where(low_half, lo_src, 0.0).astype(BF16)
            kvd_ref[:, base + LANES:base + 2 * LANES] = jnp.where(low_half, 0.0, hi_src).astype(BF16)
    for dst, c in enumerate((3, 4, 5, 8, 9, 10)):
        rest_ref[:, dst * CHUNK:(dst + 1) * CHUNK] = proj(c).astype(BF16)
    z_ref[...] = (proj(6) * proj(7)).astype(BF16)


def _in_proj(x, g, w, layer, cos, sin, seq):
    n_tok = x.shape[0]
    t = PROJ_TILE
    tps = seq // t
    tok = lambda width: pl.BlockSpec((t, width), lambda i: (i, 0))
    return pl.pallas_call(
        _in_proj_kernel,
        grid=(n_tok // t,),
        in_specs=[tok(D_MODEL),
                  pl.BlockSpec((None, 1, D_MODEL), lambda i: (layer, 0, 0)),
                  pl.BlockSpec((None, D_MODEL, D_IN), lambda i: (layer, 0, 0), pipeline_mode=pl.Buffered(1)),
                  pl.BlockSpec((t, LANES), lambda i: (i % tps, 0)),
                  pl.BlockSpec((t, LANES), lambda i: (i % tps, 0))],
        out_specs=[tok(ATTN_WIDTH), tok(KVD_WIDTH), tok(CONV_WIDTH), tok(6 * CHUNK)],
        out_shape=[jax.ShapeDtypeStruct((n_tok, ATTN_WIDTH), BF16),
                   jax.ShapeDtypeStruct((n_tok, KVD_WIDTH), BF16),
                   jax.ShapeDtypeStruct((n_tok, CONV_WIDTH), BF16),
                   jax.ShapeDtypeStruct((n_tok, 6 * CHUNK), BF16)],
        compiler_params=pltpu.CompilerParams(
            dimension_semantics=("parallel",), vmem_limit_bytes=VMEM_LIMIT_BYTES),
        name="in_proj",
    )(x, g, w, cos, sin)


def _mixer_kernel(sink_ref, x_ref, q_ref, kvp_ref, kvc_ref, kvn_ref, zp_ref, zc_ref, zn_ref, rest_ref,
                  mkv_ref, convw_ref, ga_ref, gc_ref, gm_ref, wout_ref, fin_ref, o_ref,
                  attn_ref, mixed_ref, *, final, tiles_per_seq):
    t = MIX_TILE
    nqb = t // WINDOW
    s_id = pl.program_id(0)
    n = pl.num_programs(0) - 1
    tile_a = jnp.minimum(s_id, n - 1)
    tile_g = jnp.maximum(s_id - 1, 0)
    first_a = tile_a % tiles_per_seq == 0
    last_a = tile_a % tiles_per_seq == tiles_per_seq - 1
    first_g = tile_g % tiles_per_seq == 0
    last_g = tile_g % tiles_per_seq == tiles_per_seq - 1

    @pl.when(s_id == 0)
    def _():
        attn_ref[...] = jnp.zeros(attn_ref.shape, F32)

    row = lax.broadcasted_iota(jnp.int32, (WINDOW, WINDOW), 0)
    col = lax.broadcasted_iota(jnp.int32, (WINDOW, WINDOW), 1)
    low_half = lax.broadcasted_iota(jnp.int32, (WINDOW, LANES), 1) < HEAD_DIM

    def kv_block(blk, cols):
        if blk < 0:
            return kvp_ref[:, cols]
        if blk >= nqb:
            return kvn_ref[:, cols]
        return kvc_ref[blk * WINDOW:(blk + 1) * WINDOW, cols]

    def out_chunk(c):
        cols = slice(c * OUT_CHUNK, (c + 1) * OUT_CHUNK)
        o_ref[:, cols] = x_ref[:, cols] + jnp.dot(mixed_ref[...], wout_ref[:, cols], preferred_element_type=F32)

    def attn_unit(attn_w, j, g):
        lo_bias = jnp.where(col >= row, 0.0, NEG)
        hi_bias = jnp.where(col <= row, 0.0, NEG)
        if j == 0:
            lo_bias = lo_bias + jnp.where(first_a, NEG, 0.0)
        if j == nqb - 1:
            hi_bias = hi_bias + jnp.where(last_a, NEG, 0.0)
        rows = slice(j * WINDOW, (j + 1) * WINDOW)

        def diag(base):
            lo = [kv_block(b, slice(base, base + LANES)) for b in (j - 1, j, j + 1)]
            hi = [kv_block(b, slice(base + LANES, base + 2 * LANES)) for b in (j - 1, j, j + 1)]
            return jnp.concatenate(lo + hi, axis=0)

        kd = diag(g * 4 * LANES)
        vd = diag(g * 4 * LANES + 2 * LANES)
        qa = q_ref[rows, g * 2 * LANES:g * 2 * LANES + LANES]
        qb = q_ref[rows, g * 2 * LANES + LANES:(g + 1) * 2 * LANES]
        qq = jnp.concatenate([qa, qb], axis=0)
        s = lax.dot_general(qq, kd, (((1,), (1,)), ((), ())), preferred_element_type=F32)
        probs = []
        inv = []
        for pair in range(2):
            pr = slice(pair * WINDOW, (pair + 1) * WINDOW)
            row_p = []
            row_inv = []
            for half in range(2):
                off = half * 3 * WINDOW
                sink = sink_ref[4 * g + 2 * pair + half]
                s0 = s[pr, off:off + WINDOW] + lo_bias
                s1 = s[pr, off + WINDOW:off + 2 * WINDOW]
                s2 = s[pr, off + 2 * WINDOW:off + 3 * WINDOW] + hi_bias
                m = jnp.max(jnp.maximum(jnp.maximum(s0, s1), s2), axis=-1, keepdims=True)
                m = jnp.maximum(m, sink)
                p0 = jnp.exp(s0 - m)
                p1 = jnp.exp(s1 - m)
                p2 = jnp.exp(s2 - m)
                l = jnp.sum(p0 + p1 + p2, axis=-1, keepdims=True) + jnp.exp(sink - m)
                row_p += [p0.astype(BF16), p1.astype(BF16), p2.astype(BF16)]
                row_inv.append(1.0 / l)
            probs.append(jnp.concatenate(row_p, axis=1))
            inv.append(jnp.where(low_half, row_inv[0], row_inv[1]))
        pp = jnp.concatenate(probs, axis=0)
        o = jnp.dot(pp, vd, preferred_element_type=F32)
        attn_w[rows, g * 2 * LANES:g * 2 * LANES + LANES] = o[:WINDOW] * inv[0]
        attn_w[rows, g * 2 * LANES + LANES:(g + 1) * 2 * LANES] = o[WINDOW:] * inv[1]

    def attn_gate(attn_r, mixed_w):
        gate_a = rest_ref[:, 0:ATTN_WIDTH].astype(F32)
        mixed_w[:, 0:ATTN_WIDTH] = (_rms(attn_r[...], ga_ref[...]) * _silu(gate_a)).astype(BF16)

    def conv_group(mixed_w):
        z = zc_ref[...].astype(F32)
        z_before = jnp.where(first_g, 0.0, zp_ref[BF16_ROWS - 1:BF16_ROWS, :].astype(F32))
        z_after = jnp.where(last_g, 0.0, zn_ref[0:1, :].astype(F32))
        trow = lax.broadcasted_iota(jnp.int32, z.shape, 0)
        z_prev = jnp.where(trow == 0, z_before, pltpu.roll(z, 1, 0))
        z_next = jnp.where(trow == t - 1, z_after, pltpu.roll(z, t - 1, 0))
        cw = convw_ref[...]
        conv = cw[0:1] * z_prev + cw[1:2] * z + cw[2:3] * z_next
        conv = rest_ref[:, 2 * CHUNK:3 * CHUNK].astype(F32) * conv
        gate_c = rest_ref[:, 3 * CHUNK:4 * CHUNK].astype(F32)
        mixed_w[:, ATTN_WIDTH:ATTN_WIDTH + CONV_WIDTH] = (
            _rms(conv, gc_ref[...]) * _silu(gate_c)).astype(BF16)

    def mem_group(mixed_w):
        xo = []
        for hd in range(N_X_HEADS):
            mq = rest_ref[:, 4 * CHUNK + hd * X_HEAD_DIM:4 * CHUNK + (hd + 1) * X_HEAD_DIM]
            mk = mkv_ref[0, :, hd * X_HEAD_DIM:(hd + 1) * X_HEAD_DIM]
            mv = mkv_ref[0, :, X_WIDTH + hd * X_HEAD_DIM:X_WIDTH + (hd + 1) * X_HEAD_DIM]
            sm = lax.dot_general(mq, mk, (((1,), (1,)), ((), ())), preferred_element_type=F32)
            sm = sm * (X_HEAD_DIM ** -0.5)
            m = jnp.max(sm, axis=-1, keepdims=True)
            p = jnp.exp(sm - m)
            l = jnp.sum(p, axis=-1, keepdims=True)
            xo.append(jnp.dot(p.astype(BF16), mv, preferred_element_type=F32) * (1.0 / l))
        xo = jnp.concatenate(xo, axis=1)
        gate_m = rest_ref[:, 5 * CHUNK:6 * CHUNK].astype(F32)
        mixed_w[:, ATTN_WIDTH + CONV_WIDTH:] = (_rms(xo, gm_ref[...]) * _silu(gate_m)).astype(BF16)

    def step(cur, old):
        attn_gate(attn_ref.at[old], mixed_ref)
        conv_group(mixed_ref)
        mem_group(mixed_ref)
        units = [(j, g) for j in range(nqb) for g in range(N_KV_HEADS)]
        n_out = D_MODEL // OUT_CHUNK
        for u, (j, g) in enumerate(units):
            for c in range(u * n_out // len(units), (u + 1) * n_out // len(units)):
                out_chunk(c)
            attn_unit(attn_ref.at[cur], j, g)
        if final:
            o_ref[...] = _rms(o_ref[...], fin_ref[...])

    for parity in range(2):
        @pl.when(s_id % 2 == parity)
        def _():
            step(parity, 1 - parity)


def _mixer(x, q, kvd, z, rest, mkv, sink, conv_w, g_attn, g_conv, g_mem, w_out, final_norm, layer, seq, *, final):
    n_tok = x.shape[0]
    t = MIX_TILE
    n = n_tok // t
    tps = seq // t
    nblk = t // WINDOW
    nz = t // BF16_ROWS
    att = lambda s: jnp.minimum(s, n - 1)
    gat = lambda s: jnp.maximum(s - 1, 0)
    tok = lambda stage, width: pl.BlockSpec((t, width), lambda s, *_: (stage(s), 0))
    per_layer = lambda r, c: pl.BlockSpec((None, r, c), lambda s, *_: (layer, 0, 0))
    grid_spec = pltpu.PrefetchScalarGridSpec(
        num_scalar_prefetch=1,
        grid=(n + 1,),
        in_specs=[
            tok(gat, D_MODEL),
            tok(att, ATTN_WIDTH),
            pl.BlockSpec((WINDOW, KVD_WIDTH), lambda s, *_: (jnp.maximum(att(s) * nblk - 1, 0), 0)),
            tok(att, KVD_WIDTH),
            pl.BlockSpec((WINDOW, KVD_WIDTH),
                         lambda s, *_: (jnp.minimum((att(s) + 1) * nblk, n_tok // WINDOW - 1), 0)),
            pl.BlockSpec((BF16_ROWS, CONV_WIDTH), lambda s, *_: (jnp.maximum(gat(s) * nz - 1, 0), 0)),
            tok(gat, CONV_WIDTH),
            pl.BlockSpec((BF16_ROWS, CONV_WIDTH),
                         lambda s, *_: (jnp.minimum((gat(s) + 1) * nz, n_tok // BF16_ROWS - 1), 0)),
            tok(gat, 6 * CHUNK),
            pl.BlockSpec((1, N_MEM, 2 * X_WIDTH), lambda s, *_: (gat(s) // tps, 0, 0)),
            per_layer(CONV_K, CONV_WIDTH),
            per_layer(1, ATTN_WIDTH), per_layer(1, CONV_WIDTH), per_layer(1, X_WIDTH),
            pl.BlockSpec((None, D_MODEL, W_OUT_PITCH), lambda s, *_: (layer, 0, 0),
                         pipeline_mode=pl.Buffered(1)),
            pl.BlockSpec((1, D_MODEL), lambda s, *_: (0, 0)),
        ],
        out_specs=tok(gat, D_MODEL),
        scratch_shapes=[pltpu.VMEM((2, t, ATTN_WIDTH), F32), pltpu.VMEM((t, D_MODEL), BF16)],
    )
    return pl.pallas_call(
        functools.partial(_mixer_kernel, final=final, tiles_per_seq=tps),
        grid_spec=grid_spec,
        out_shape=jax.ShapeDtypeStruct((n_tok, D_MODEL), F32),
        compiler_params=pltpu.CompilerParams(
            dimension_semantics=("arbitrary",), vmem_limit_bytes=VMEM_LIMIT_BYTES),
        name="mixer",
    )(sink, x, q, kvd, kvd, kvd, z, z, z, rest, mkv, conv_w, g_attn, g_conv, g_mem, w_out, final_norm)


def _trunk(x, mem, cos, sin, norm_in, w_in, attn_sink, conv_w, norm_mem, w_mem_kv,
           g_attn, g_conv, g_mem, w_out, final_norm):
    depth = w_in.shape[0]
    b, seq, _ = x.shape
    cos, sin = cos[:seq], sin[:seq]
    x = x.reshape(b * seq, D_MODEL)
    for l in range(depth):
        mkv = _mem_kv(mem, norm_mem, w_mem_kv, l)
        q, kvd, z, rest = _in_proj(x, norm_in, w_in, l, cos, sin, seq)
        x = _mixer(x, q, kvd, z, rest, mkv, attn_sink[l], conv_w, g_attn, g_conv, g_mem, w_out, final_norm,
                   l, seq, final=(l == depth - 1))
    return x.reshape(b, seq, D_MODEL)


def kernel(x_prompt, x_sample, mem_prompt, mem_sample, norm_in, w_in, attn_sink, conv_w, norm_mem, w_mem_kv,
           g_attn, g_conv, g_mem, w_out, final_norm):
    depth = w_in.shape[0]
    cos, sin = _rope_tables(max(x_prompt.shape[1], x_sample.shape[1]))
    weights = (norm_in.reshape(depth, 1, D_MODEL), w_in.astype(BF16), attn_sink, conv_w,
               norm_mem.reshape(depth, 1, D_MODEL), w_mem_kv.astype(BF16),
               g_attn.reshape(depth, 1, ATTN_WIDTH), g_conv.reshape(depth, 1, CONV_WIDTH),
               g_mem.reshape(depth, 1, X_WIDTH),
               jnp.pad(w_out.astype(BF16), ((0, 0), (0, 0), (0, W_OUT_PITCH - D_MODEL))),
               final_norm.reshape(1, D_MODEL))
    y_prompt = _trunk(x_prompt, mem_prompt, cos, sin, *weights)
    y_sample = _trunk(x_sample, mem_sample, cos, sin, *weights)
    return (y_prompt, y_sample)
```

```python
import functools

import jax
import jax.numpy as jnp
from jax import lax
from jax.experimental import pallas as pl
from jax.experimental.pallas import tpu as pltpu

D_MODEL = 2048
HEAD_DIM = 64
N_Q_HEADS = 16
N_KV_HEADS = 4
ATTN_WIDTH = N_Q_HEADS * HEAD_DIM
KV_WIDTH = N_KV_HEADS * HEAD_DIM
WINDOW = 128
CONV_WIDTH = 512
CONV_K = 3
N_MEM = 256
N_X_HEADS = 4
X_WIDTH = 512
X_HEAD_DIM = X_WIDTH // N_X_HEADS
ROPE_THETA = 10000.0
EPS = 1e-6
D_IN = 5632
CHUNK = 512

LANES = 128
F32_ROWS = 8
VMEM_LIMIT_BYTES = 56 * 1024 * 1024
NEG = -1e30
KVD_WIDTH = N_KV_HEADS * 4 * LANES
CM_WIDTH = CONV_WIDTH + X_WIDTH

ROPE_TILE = 2048
PROJ_TILE = 512
MIX_TILE = 256
OUT_CHUNK = 256
W_OUT_PITCH = D_MODEL + LANES

BF16 = jnp.bfloat16
F32 = jnp.float32


def _rms(x, g):
    ms = jnp.mean(x * x, axis=-1, keepdims=True)
    return (x * lax.rsqrt(ms + EPS)) * g


def _silu(x):
    return x / (1.0 + jnp.exp(-x))


def _rope_table_kernel(inv_ref, cos_ref, sin_ref):
    i = pl.program_id(0)
    tile = cos_ref.shape[0]
    rows = lax.broadcasted_iota(jnp.int32, (tile, LANES), 0) + i * tile
    lane = lax.broadcasted_iota(jnp.int32, (tile, LANES), 1)
    ang = rows.astype(F32) * inv_ref[...]
    cos_ref[...] = jnp.cos(ang)
    sin_ref[...] = jnp.where((lane % HEAD_DIM) < HEAD_DIM // 2, -jnp.sin(ang), jnp.sin(ang))


def _rope_tables(seq):
    inv_freq = ROPE_THETA ** (-jnp.arange(0, HEAD_DIM, 2, dtype=jnp.float32) / HEAD_DIM)
    inv = jnp.tile(inv_freq, LANES // (HEAD_DIM // 2)).reshape(1, LANES)
    tile = min(ROPE_TILE, seq)
    return pl.pallas_call(
        _rope_table_kernel,
        grid=(seq // tile,),
        in_specs=[pl.BlockSpec((1, LANES), lambda i: (0, 0))],
        out_specs=[pl.BlockSpec((tile, LANES), lambda i: (i, 0))] * 2,
        out_shape=[jax.ShapeDtypeStruct((seq, LANES), F32)] * 2,
        name="rope_tables",
    )(inv)


def _mem_kv_kernel(mem_ref, g_ref, w_ref, o_ref):
    h = _rms(mem_ref[0], g_ref[...]).astype(BF16)
    o_ref[0] = jnp.dot(h, w_ref[...], preferred_element_type=F32).astype(BF16)


def _mem_kv(mem, g, w, layer):
    b = mem.shape[0]
    return pl.pallas_call(
        _mem_kv_kernel,
        grid=(b,),
        in_specs=[pl.BlockSpec((1, N_MEM, D_MODEL), lambda i: (i, 0, 0)),
                  pl.BlockSpec((None, 1, D_MODEL), lambda i: (layer, 0, 0)),
                  pl.BlockSpec((None, D_MODEL, 2 * X_WIDTH), lambda i: (layer, 0, 0))],
        out_specs=pl.BlockSpec((1, N_MEM, 2 * X_WIDTH), lambda i: (i, 0, 0)),
        out_shape=jax.ShapeDtypeStruct((b, N_MEM, 2 * X_WIDTH), BF16),
        name="mem_kv",
    )(mem, g, w)


def _swap_halves(x):
    lane = lax.broadcasted_iota(jnp.int32, x.shape, 1)
    first = (lane % HEAD_DIM) < HEAD_DIM // 2
    return jnp.where(first, pltpu.roll(x, LANES - HEAD_DIM // 2, 1), pltpu.roll(x, HEAD_DIM // 2, 1))


def _in_proj_kernel(x_ref, xp_ref, xn_ref, g_ref, w_ref, cos_ref, sin_ref, mkv_ref, convw_ref, gc_ref, gm_ref,
                    q_ref, kvd_ref, sga_ref, cm_ref, *, tiles_per_seq):
    t = PROJ_TILE
    i = pl.program_id(0)
    first = i % tiles_per_seq == 0
    last = i % tiles_per_seq == tiles_per_seq - 1
    g = g_ref[...]
    h = _rms(x_ref[...], g).astype(BF16)
    halo = jnp.concatenate([xp_ref[...], xn_ref[...]], axis=0)
    h_ext = jnp.concatenate([h, _rms(halo, g).astype(BF16)], axis=0)
    cos = cos_ref[...]
    sin = sin_ref[...]
    low_half = lax.broadcasted_iota(jnp.int32, cos.shape, 1) < HEAD_DIM

    def proj(c, lhs=h):
        return jnp.dot(lhs, w_ref[:, c * CHUNK:(c + 1) * CHUNK], preferred_element_type=F32)

    def rope(xc):
        return xc * cos + _swap_halves(xc) * sin

    mq = proj(9).astype(BF16)
    xo = []
    for hd in range(N_X_HEADS):
        mk = mkv_ref[0, :, hd * X_HEAD_DIM:(hd + 1) * X_HEAD_DIM]
        mv = mkv_ref[0, :, X_WIDTH + hd * X_HEAD_DIM:X_WIDTH + (hd + 1) * X_HEAD_DIM]
        sm = lax.dot_general(mq[:, hd * X_HEAD_DIM:(hd + 1) * X_HEAD_DIM], mk, (((1,), (1,)), ((), ())),
                             preferred_element_type=F32)
        sm = sm * (X_HEAD_DIM ** -0.5)
        m = jnp.max(sm, axis=-1, keepdims=True)
        p = jnp.exp(sm - m)
        l = jnp.sum(p, axis=-1, keepdims=True)
        xo.append(jnp.dot(p.astype(BF16), mv, preferred_element_type=F32) * (1.0 / l))
    xo = jnp.concatenate(xo, axis=1)
    cm_ref[:, CONV_WIDTH:] = (_rms(xo, gm_ref[...]) * _silu(proj(10))).astype(BF16)

    zc = proj(6, h_ext) * proj(7, h_ext)
    z = zc[:t]
    z_before = jnp.where(first, 0.0, zc[t + F32_ROWS - 1:t + F32_ROWS])
    z_after = jnp.where(last, 0.0, zc[t + F32_ROWS:t + F32_ROWS + 1])
    trow = lax.broadcasted_iota(jnp.int32, z.shape, 0)
    z_prev = jnp.where(trow == 0, z_before, pltpu.roll(z, 1, 0))
    z_next = jnp.where(trow == t - 1, z_after, pltpu.roll(z, t - 1, 0))
    cw = convw_ref[...]
    conv = proj(5) * (cw[0:1] * z_prev + cw[1:2] * z + cw[2:3] * z_next)
    cm_ref[:, 0:CONV_WIDTH] = (_rms(conv, gc_ref[...]) * _silu(proj(8))).astype(BF16)

    for dst, c in enumerate((3, 4)):
        sga_ref[:, dst * CHUNK:(dst + 1) * CHUNK] = _silu(proj(c)).astype(BF16)

    pc = proj(2)
    for s in range(CHUNK // LANES):
        xc = pc[:, s * LANES:(s + 1) * LANES]
        is_v = s >= KV_WIDTH // LANES
        c = s - KV_WIDTH // LANES if is_v else s
        if not is_v:
            xc = rope(xc)
        sw = pltpu.roll(xc, HEAD_DIM, 1)
        for head, (lo_src, hi_src) in ((2 * c, (xc, sw)), (2 * c + 1, (sw, xc))):
            base = head * 4 * LANES + (2 * LANES if is_v else 0)
            kvd_ref[:, base:base + LANES] = jnp.where(low_half, lo_src, 0.0).astype(BF16)
            kvd_ref[:, base + LANES:base + 2 * LANES] = jnp.where(low_half, 0.0, hi_src).astype(BF16)
    for c in range(2):
        pc = proj(c)
        for s in range(CHUNK // LANES):
            xc = pc[:, s * LANES:(s + 1) * LANES]
            q_ref[:, c * CHUNK + s * LANES:c * CHUNK + (s + 1) * LANES] = (
                rope(xc) * (HEAD_DIM ** -0.5)).astype(BF16)


def _in_proj(x, mkv, g, w, conv_w, g_conv, g_mem, layer, cos, sin, seq):
    n_tok = x.shape[0]
    t = PROJ_TILE
    tps = seq // t
    nh = t // F32_ROWS
    tok = lambda width: pl.BlockSpec((t, width), lambda i: (i, 0))
    per_layer = lambda r, c: pl.BlockSpec((None, r, c), lambda i: (layer, 0, 0))
    return pl.pallas_call(
        functools.partial(_in_proj_kernel, tiles_per_seq=tps),
        grid=(n_tok // t,),
        in_specs=[tok(D_MODEL),
                  pl.BlockSpec((F32_ROWS, D_MODEL), lambda i: (jnp.maximum(i * nh - 1, 0), 0)),
                  pl.BlockSpec((F32_ROWS, D_MODEL), lambda i: (jnp.minimum((i + 1) * nh, n_tok // F32_ROWS - 1), 0)),
                  per_layer(1, D_MODEL),
                  pl.BlockSpec((None, D_MODEL, D_IN), lambda i: (layer, 0, 0), pipeline_mode=pl.Buffered(1)),
                  pl.BlockSpec((t, LANES), lambda i: (i % tps, 0)),
                  pl.BlockSpec((t, LANES), lambda i: (i % tps, 0)),
                  pl.BlockSpec((1, N_MEM, 2 * X_WIDTH), lambda i: (i // tps, 0, 0)),
                  per_layer(CONV_K, CONV_WIDTH), per_layer(1, CONV_WIDTH), per_layer(1, X_WIDTH)],
        out_specs=[tok(ATTN_WIDTH), tok(KVD_WIDTH), tok(ATTN_WIDTH), tok(CM_WIDTH)],
        out_shape=[jax.ShapeDtypeStruct((n_tok, ATTN_WIDTH), BF16),
                   jax.ShapeDtypeStruct((n_tok, KVD_WIDTH), BF16),
                   jax.ShapeDtypeStruct((n_tok, ATTN_WIDTH), BF16),
                   jax.ShapeDtypeStruct((n_tok, CM_WIDTH), BF16)],
        compiler_params=pltpu.CompilerParams(
            dimension_semantics=("parallel",), vmem_limit_bytes=VMEM_LIMIT_BYTES),
        name="in_proj",
    )(x, x, x, g, w, cos, sin, mkv, conv_w, g_conv, g_mem)


def _mixer_kernel(sink_ref, x_ref, q_ref, kvp_ref, kvc_ref, kvn_ref, sga_ref, cm_ref, ga_ref, wout_ref, fin_ref,
                  o_ref, attn_ref, mixed_ref, *, final, tiles_per_seq):
    t = MIX_TILE
    nqb = t // WINDOW
    s_id = pl.program_id(0)
    n = pl.num_programs(0) - 1
    tile_a = jnp.minimum(s_id, n - 1)
    first_a = tile_a % tiles_per_seq == 0
    last_a = tile_a % tiles_per_seq == tiles_per_seq - 1

    @pl.when(s_id == 0)
    def _():
        attn_ref[...] = jnp.zeros(attn_ref.shape, F32)

    row = lax.broadcasted_iota(jnp.int32, (WINDOW, WINDOW), 0)
    col = lax.broadcasted_iota(jnp.int32, (WINDOW, WINDOW), 1)
    low_half = lax.broadcasted_iota(jnp.int32, (WINDOW, LANES), 1) < HEAD_DIM

    def kv_block(blk, cols):
        if blk < 0:
            return kvp_ref[:, cols]
        if blk >= nqb:
            return kvn_ref[:, cols]
        return kvc_ref[blk * WINDOW:(blk + 1) * WINDOW, cols]

    def out_chunk(c):
        cols = slice(c * OUT_CHUNK, (c + 1) * OUT_CHUNK)
        o_ref[:, cols] = x_ref[:, cols] + jnp.dot(mixed_ref[...], wout_ref[:, cols], preferred_element_type=F32)

    def attn_unit(attn_w, j, g):
        lo_bias = jnp.where(col >= row, 0.0, NEG)
        hi_bias = jnp.where(col <= row, 0.0, NEG)
        if j == 0:
            lo_bias = lo_bias + jnp.where(first_a, NEG, 0.0)
        if j == nqb - 1:
            hi_bias = hi_bias + jnp.where(last_a, NEG, 0.0)
        rows = slice(j * WINDOW, (j + 1) * WINDOW)

        def diag(base):
            lo = [kv_block(b, slice(base, base + LANES)) for b in (j - 1, j, j + 1)]
            hi = [kv_block(b, slice(base + LANES, base + 2 * LANES)) for b in (j - 1, j, j + 1)]
            return jnp.concatenate(lo + hi, axis=0)

        kd = diag(g * 4 * LANES)
        vd = diag(g * 4 * LANES + 2 * LANES)
        qa = q_ref[rows, g * 2 * LANES:g * 2 * LANES + LANES]
        qb = q_ref[rows, g * 2 * LANES + LANES:(g + 1) * 2 * LANES]
        qq = jnp.concatenate([qa, qb], axis=0)
        s = lax.dot_general(qq, kd, (((1,), (1,)), ((), ())), preferred_element_type=F32)
        probs = []
        inv = []
        for pair in range(2):
            pr = slice(pair * WINDOW, (pair + 1) * WINDOW)
            row_p = []
            row_inv = []
            for half in range(2):
                off = half * 3 * WINDOW
                sink = sink_ref[4 * g + 2 * pair + half]
                s0 = s[pr, off:off + WINDOW] + lo_bias
                s1 = s[pr, off + WINDOW:off + 2 * WINDOW]
                s2 = s[pr, off + 2 * WINDOW:off + 3 * WINDOW] + hi_bias
                m = jnp.max(jnp.maximum(jnp.maximum(s0, s1), s2), axis=-1, keepdims=True)
                m = jnp.maximum(m, sink)
                p0 = jnp.exp(s0 - m)
                p1 = jnp.exp(s1 - m)
                p2 = jnp.exp(s2 - m)
                l = jnp.sum(p0 + p1 + p2, axis=-1, keepdims=True) + jnp.exp(sink - m)
                row_p += [p0.astype(BF16), p1.astype(BF16), p2.astype(BF16)]
                row_inv.append(1.0 / l)
            probs.append(jnp.concatenate(row_p, axis=1))
            inv.append(jnp.where(low_half, row_inv[0], row_inv[1]))
        pp = jnp.concatenate(probs, axis=0)
        o = jnp.dot(pp, vd, preferred_element_type=F32)
        attn_w[rows, g * 2 * LANES:g * 2 * LANES + LANES] = o[:WINDOW] * inv[0]
        attn_w[rows, g * 2 * LANES + LANES:(g + 1) * 2 * LANES] = o[WINDOW:] * inv[1]

    def step(cur, old):
        mixed_ref[:, 0:ATTN_WIDTH] = (
            _rms(attn_ref[old], ga_ref[...]) * sga_ref[...].astype(F32)).astype(BF16)
        mixed_ref[:, ATTN_WIDTH:] = cm_ref[...]
        units = [(j, g) for j in range(nqb) for g in range(N_KV_HEADS)]
        n_out = D_MODEL // OUT_CHUNK
        for u, (j, g) in enumerate(units):
            for c in range(u * n_out // len(units), (u + 1) * n_out // len(units)):
                out_chunk(c)
            attn_unit(attn_ref.at[cur], j, g)
        if final:
            o_ref[...] = _rms(o_ref[...], fin_ref[...])

    for parity in range(2):
        @pl.when(s_id % 2 == parity)
        def _():
            step(parity, 1 - parity)


def _mixer(x, q, kvd, sga, cm, sink, g_attn, w_out, final_norm, layer, seq, *, final):
    n_tok = x.shape[0]
    t = MIX_TILE
    n = n_tok // t
    tps = seq // t
    nblk = t // WINDOW
    att = lambda s: jnp.minimum(s, n - 1)
    gat = lambda s: jnp.maximum(s - 1, 0)
    tok = lambda stage, width: pl.BlockSpec((t, width), lambda s, *_: (stage(s), 0))
    grid_spec = pltpu.PrefetchScalarGridSpec(
        num_scalar_prefetch=1,
        grid=(n + 1,),
        in_specs=[
            tok(gat, D_MODEL),
            tok(att, ATTN_WIDTH),
            pl.BlockSpec((WINDOW, KVD_WIDTH), lambda s, *_: (jnp.maximum(att(s) * nblk - 1, 0), 0)),
            tok(att, KVD_WIDTH),
            pl.BlockSpec((WINDOW, KVD_WIDTH),
                         lambda s, *_: (jnp.minimum((att(s) + 1) * nblk, n_tok // WINDOW - 1), 0)),
            tok(gat, ATTN_WIDTH),
            tok(gat, CM_WIDTH),
            pl.BlockSpec((None, 1, ATTN_WIDTH), lambda s, *_: (layer, 0, 0)),
            pl.BlockSpec((None, D_MODEL, W_OUT_PITCH), lambda s, *_: (layer, 0, 0),
                         pipeline_mode=pl.Buffered(1)),
            pl.BlockSpec((1, D_MODEL), lambda s, *_: (0, 0)),
        ],
        out_specs=tok(gat, D_MODEL),
        scratch_shapes=[pltpu.VMEM((2, t, ATTN_WIDTH), F32), pltpu.VMEM((t, D_MODEL), BF16)],
    )
    return pl.pallas_call(
        functools.partial(_mixer_kernel, final=final, tiles_per_seq=tps),
        grid_spec=grid_spec,
        out_shape=jax.ShapeDtypeStruct((n_tok, D_MODEL), F32),
        compiler_params=pltpu.CompilerParams(
            dimension_semantics=("arbitrary",), vmem_limit_bytes=VMEM_LIMIT_BYTES),
        name="mixer",
    )(sink, x, q, kvd, kvd, kvd, sga, cm, g_attn, w_out, final_norm)


def _trunk(x, mem, cos, sin, norm_in, w_in, attn_sink, conv_w, norm_mem, w_mem_kv,
           g_attn, g_conv, g_mem, w_out, final_norm):
    depth = w_in.shape[0]
    b, seq, _ = x.shape
    cos, sin = cos[:seq], sin[:seq]
    x = x.reshape(b * seq, D_MODEL)
    for l in range(depth):
        mkv = _mem_kv(mem, norm_mem, w_mem_kv, l)
        q, kvd, sga, cm = _in_proj(x, mkv, norm_in, w_in, conv_w, g_conv, g_mem, l, cos, sin, seq)
        x = _mixer(x, q, kvd, sga, cm, attn_sink[l], g_attn, w_out, final_norm, l, seq, final=(l == depth - 1))
    return x.reshape(b, seq, D_MODEL)


def kernel(x_prompt, x_sample, mem_prompt, mem_sample, norm_in, w_in, attn_sink, conv_w, norm_mem, w_mem_kv,
           g_attn, g_conv, g_mem, w_out, final_norm):
    depth = w_in.shape[0]
    cos, sin = _rope_tables(max(x_prompt.shape[1], x_sample.shape[1]))
    weights = (norm_in.reshape(depth, 1, D_MODEL), w_in.astype(BF16), attn_sink, conv_w,
               norm_mem.reshape(depth, 1, D_MODEL), w_mem_kv.astype(BF16),
               g_attn.reshape(depth, 1, ATTN_WIDTH), g_conv.reshape(depth, 1, CONV_WIDTH),
               g_mem.reshape(depth, 1, X_WIDTH),
               jnp.pad(w_out.astype(BF16), ((0, 0), (0, 0), (0, W_OUT_PITCH - D_MODEL))),
               final_norm.reshape(1, D_MODEL))
    y_prompt = _trunk(x_prompt, mem_prompt, cos, sin, *weights)
    y_sample = _trunk(x_sample, mem_sample, cos, sin, *weights)
    return (y_prompt, y_sample)
```

```python
import functools

import jax
import jax.numpy as jnp
from jax import lax
from jax.experimental import pallas as pl
from jax.experimental.pallas import tpu as pltpu

D_MODEL = 2048
HEAD_DIM = 64
N_Q_HEADS = 16
N_KV_HEADS = 4
ATTN_WIDTH = N_Q_HEADS * HEAD_DIM
KV_WIDTH = N_KV_HEADS * HEAD_DIM
WINDOW = 128
CONV_WIDTH = 512
CONV_K = 3
N_MEM = 256
N_X_HEADS = 4
X_WIDTH = 512
X_HEAD_DIM = X_WIDTH // N_X_HEADS
ROPE_THETA = 10000.0
EPS = 1e-6
D_IN = 5632
CHUNK = 512

LANES = 128
F32_ROWS = 8
VMEM_LIMIT_BYTES = 56 * 1024 * 1024
LOG2E = 1.4426950408889634
NEG = -1e30
KVD_WIDTH = N_KV_HEADS * 4 * LANES
CM_WIDTH = CONV_WIDTH + X_WIDTH

ROPE_TILE = 2048
PROJ_TILE = 512
MIX_TILE = 256
OUT_CHUNK = 256
W_OUT_PITCH = D_MODEL + LANES

BF16 = jnp.bfloat16
F32 = jnp.float32


def _rms(x, g):
    ms = jnp.mean(x * x, axis=-1, keepdims=True)
    return (x * lax.rsqrt(ms + EPS)) * g


def _silu(x):
    return x / (1.0 + jnp.exp(-x))


def _rope_table_kernel(inv_ref, cos_ref, sin_ref, base_cos, base_sin):
    i = pl.program_id(0)
    nblk = cos_ref.shape[0] // WINDOW
    inv = inv_ref[...]

    @pl.when(i == 0)
    def _():
        ang = lax.broadcasted_iota(jnp.int32, (WINDOW, LANES), 0).astype(F32) * inv
        base_cos[...] = jnp.cos(ang)
        base_sin[...] = jnp.sin(ang)

    first_row = (lax.broadcasted_iota(jnp.int32, (nblk, LANES), 0) + i * nblk) * WINDOW
    ang0 = first_row.astype(F32) * inv
    cos0 = jnp.cos(ang0)
    sin0 = jnp.sin(ang0)
    lane = lax.broadcasted_iota(jnp.int32, (WINDOW, LANES), 1)
    first_half = (lane % HEAD_DIM) < HEAD_DIM // 2
    cb = base_cos[...]
    sb = base_sin[...]
    for b in range(nblk):
        c0 = cos0[b:b + 1, :]
        s0 = sin0[b:b + 1, :]
        cos_ref[b * WINDOW:(b + 1) * WINDOW, :] = c0 * cb - s0 * sb
        sn = s0 * cb + c0 * sb
        sin_ref[b * WINDOW:(b + 1) * WINDOW, :] = jnp.where(first_half, -sn, sn)


def _rope_tables(seq):
    inv_freq = ROPE_THETA ** (-jnp.arange(0, HEAD_DIM, 2, dtype=jnp.float32) / HEAD_DIM)
    inv = jnp.tile(inv_freq, LANES // (HEAD_DIM // 2)).reshape(1, LANES)
    tile = min(ROPE_TILE, seq)
    return pl.pallas_call(
        _rope_table_kernel,
        grid=(seq // tile,),
        in_specs=[pl.BlockSpec((1, LANES), lambda i: (0, 0))],
        out_specs=[pl.BlockSpec((tile, LANES), lambda i: (i, 0))] * 2,
        out_shape=[jax.ShapeDtypeStruct((seq, LANES), F32)] * 2,
        scratch_shapes=[pltpu.VMEM((WINDOW, LANES), F32)] * 2,
        compiler_params=pltpu.CompilerParams(dimension_semantics=("arbitrary",)),
        name="rope_tables",
    )(inv)


def _mem_kv_kernel(mem_ref, g_ref, w_ref, o_ref):
    h = _rms(mem_ref[0], g_ref[...]).astype(BF16)
    o_ref[0] = jnp.dot(h, w_ref[...], preferred_element_type=F32).astype(BF16)


def _mem_kv(mem, g, w, layer):
    b = mem.shape[0]
    return pl.pallas_call(
        _mem_kv_kernel,
        grid=(b,),
        in_specs=[pl.BlockSpec((1, N_MEM, D_MODEL), lambda i: (i, 0, 0)),
                  pl.BlockSpec((None, 1, D_MODEL), lambda i: (layer, 0, 0)),
                  pl.BlockSpec((None, D_MODEL, 2 * X_WIDTH), lambda i: (layer, 0, 0))],
        out_specs=pl.BlockSpec((1, N_MEM, 2 * X_WIDTH), lambda i: (i, 0, 0)),
        out_shape=jax.ShapeDtypeStruct((b, N_MEM, 2 * X_WIDTH), BF16),
        name="mem_kv",
    )(mem, g, w)


def _swap_halves(x):
    lane = lax.broadcasted_iota(jnp.int32, x.shape, 1)
    first = (lane % HEAD_DIM) < HEAD_DIM // 2
    return jnp.where(first, pltpu.roll(x, LANES - HEAD_DIM // 2, 1), pltpu.roll(x, HEAD_DIM // 2, 1))


def _in_proj_kernel(x_ref, xp_ref, xn_ref, g_ref, w_ref, cos_ref, sin_ref, mkv_ref, convw_ref, gc_ref, gm_ref,
                    q_ref, kvd_ref, sga_ref, cm_ref, *, tiles_per_seq):
    t = PROJ_TILE
    i = pl.program_id(0)
    first = i % tiles_per_seq == 0
    last = i % tiles_per_seq == tiles_per_seq - 1
    g = g_ref[...]
    h = _rms(x_ref[...], g).astype(BF16)
    halo = jnp.concatenate([xp_ref[...], xn_ref[...]], axis=0)
    h_ext = jnp.concatenate([h, _rms(halo, g).astype(BF16)], axis=0)
    cos = cos_ref[...]
    sin = sin_ref[...]
    low_half = lax.broadcasted_iota(jnp.int32, cos.shape, 1) < HEAD_DIM

    def proj(c, lhs=h):
        return jnp.dot(lhs, w_ref[:, c * CHUNK:(c + 1) * CHUNK], preferred_element_type=F32)

    def rope(xc):
        return xc * cos + _swap_halves(xc) * sin

    mq = proj(9).astype(BF16)
    xo = []
    for hd in range(N_X_HEADS):
        mk = mkv_ref[0, :, hd * X_HEAD_DIM:(hd + 1) * X_HEAD_DIM]
        mv = mkv_ref[0, :, X_WIDTH + hd * X_HEAD_DIM:X_WIDTH + (hd + 1) * X_HEAD_DIM]
        sm = lax.dot_general(mq[:, hd * X_HEAD_DIM:(hd + 1) * X_HEAD_DIM], mk, (((1,), (1,)), ((), ())),
                             preferred_element_type=F32)
        sm = sm * (X_HEAD_DIM ** -0.5)
        m = jnp.max(sm, axis=-1, keepdims=True)
        p = jnp.exp(sm - m)
        l = jnp.sum(p, axis=-1, keepdims=True)
        xo.append(jnp.dot(p.astype(BF16), mv, preferred_element_type=F32) * (1.0 / l))
    xo = jnp.concatenate(xo, axis=1)
    cm_ref[:, CONV_WIDTH:] = (_rms(xo, gm_ref[...]) * _silu(proj(10))).astype(BF16)

    zc = proj(6, h_ext) * proj(7, h_ext)
    z = zc[:t]
    z_before = jnp.where(first, 0.0, zc[t + F32_ROWS - 1:t + F32_ROWS])
    z_after = jnp.where(last, 0.0, zc[t + F32_ROWS:t + F32_ROWS + 1])
    trow = lax.broadcasted_iota(jnp.int32, z.shape, 0)
    z_prev = jnp.where(trow == 0, z_before, pltpu.roll(z, 1, 0))
    z_next = jnp.where(trow == t - 1, z_after, pltpu.roll(z, t - 1, 0))
    cw = convw_ref[...]
    conv = proj(5) * (cw[0:1] * z_prev + cw[1:2] * z + cw[2:3] * z_next)
    cm_ref[:, 0:CONV_WIDTH] = (_rms(conv, gc_ref[...]) * _silu(proj(8))).astype(BF16)

    pc = proj(2)
    for s in range(CHUNK // LANES):
        xc = pc[:, s * LANES:(s + 1) * LANES]
        is_v = s >= KV_WIDTH // LANES
        c = s - KV_WIDTH // LANES if is_v else s
        if not is_v:
            xc = rope(xc)
        sw = pltpu.roll(xc, HEAD_DIM, 1)
        for head, (lo_src, hi_src) in ((2 * c, (xc, sw)), (2 * c + 1, (sw, xc))):
            base = head * 4 * LANES + (2 * LANES if is_v else 0)
            kvd_ref[:, base:base + LANES] = jnp.where(low_half, lo_src, 0.0).astype(BF16)
            kvd_ref[:, base + LANES:base + 2 * LANES] = jnp.where(low_half, 0.0, hi_src).astype(BF16)
    for c in range(2):
        pc = proj(c)
        for s in range(CHUNK // LANES):
            xc = pc[:, s * LANES:(s + 1) * LANES]
            q_ref[:, c * CHUNK + s * LANES:c * CHUNK + (s + 1) * LANES] = (
                rope(xc) * (HEAD_DIM ** -0.5 * LOG2E)).astype(BF16)
    for dst, c in enumerate((3, 4)):
        sga_ref[:, dst * CHUNK:(dst + 1) * CHUNK] = _silu(proj(c)).astype(BF16)


def _in_proj(x, mkv, g, w, conv_w, g_conv, g_mem, layer, cos, sin, seq):
    n_tok = x.shape[0]
    t = PROJ_TILE
    tps = seq // t
    nh = t // F32_ROWS
    tok = lambda width: pl.BlockSpec((t, width), lambda i: (i, 0))
    per_layer = lambda r, c: pl.BlockSpec((None, r, c), lambda i: (layer, 0, 0))
    return pl.pallas_call(
        functools.partial(_in_proj_kernel, tiles_per_seq=tps),
        grid=(n_tok // t,),
        in_specs=[tok(D_MODEL),
                  pl.BlockSpec((F32_ROWS, D_MODEL), lambda i: (jnp.maximum(i * nh - 1, 0), 0)),
                  pl.BlockSpec((F32_ROWS, D_MODEL), lambda i: (jnp.minimum((i + 1) * nh, n_tok // F32_ROWS - 1), 0)),
                  per_layer(1, D_MODEL),
                  pl.BlockSpec((None, D_MODEL, D_IN), lambda i: (layer, 0, 0), pipeline_mode=pl.Buffered(1)),
                  pl.BlockSpec((t, LANES), lambda i: (i % tps, 0)),
                  pl.BlockSpec((t, LANES), lambda i: (i % tps, 0)),
                  pl.BlockSpec((1, N_MEM, 2 * X_WIDTH), lambda i: (i // tps, 0, 0)),
                  per_layer(CONV_K, CONV_WIDTH), per_layer(1, CONV_WIDTH), per_layer(1, X_WIDTH)],
        out_specs=[tok(ATTN_WIDTH), tok(KVD_WIDTH), tok(ATTN_WIDTH), tok(CM_WIDTH)],
        out_shape=[jax.ShapeDtypeStruct((n_tok, ATTN_WIDTH), BF16),
                   jax.ShapeDtypeStruct((n_tok, KVD_WIDTH), BF16),
                   jax.ShapeDtypeStruct((n_tok, ATTN_WIDTH), BF16),
                   jax.ShapeDtypeStruct((n_tok, CM_WIDTH), BF16)],
        compiler_params=pltpu.CompilerParams(
            dimension_semantics=("parallel",), vmem_limit_bytes=VMEM_LIMIT_BYTES),
        name="in_proj",
    )(x, x, x, g, w, cos, sin, mkv, conv_w, g_conv, g_mem)


def _mixer_kernel(sink_ref, x_ref, q_ref, kvp_ref, kvc_ref, kvn_ref, sga_ref, cm_ref, ga_ref, wout_ref, fin_ref,
                  o_ref, attn_ref, mixed_ref, *, final, tiles_per_seq):
    t = MIX_TILE
    nqb = t // WINDOW
    s_id = pl.program_id(0)
    n = pl.num_programs(0) - 1
    tile_a = jnp.minimum(s_id, n - 1)
    first_a = tile_a % tiles_per_seq == 0
    last_a = tile_a % tiles_per_seq == tiles_per_seq - 1

    @pl.when(s_id == 0)
    def _():
        attn_ref[...] = jnp.zeros(attn_ref.shape, F32)

    row = lax.broadcasted_iota(jnp.int32, (WINDOW, WINDOW), 0)
    col = lax.broadcasted_iota(jnp.int32, (WINDOW, WINDOW), 1)
    low_half = lax.broadcasted_iota(jnp.int32, (WINDOW, LANES), 1) < HEAD_DIM
    ones_lo = jnp.where(low_half, 1.0, 0.0).astype(BF16)
    ones_hi = jnp.where(low_half, 0.0, 1.0).astype(BF16)
    head_ones = jnp.concatenate([ones_lo] * 3 + [ones_hi] * 3, axis=0)

    def kv_block(blk, cols):
        if blk < 0:
            return kvp_ref[:, cols]
        if blk >= nqb:
            return kvn_ref[:, cols]
        return kvc_ref[blk * WINDOW:(blk + 1) * WINDOW, cols]

    def out_chunk(c):
        cols = slice(c * OUT_CHUNK, (c + 1) * OUT_CHUNK)
        o_ref[:, cols] = x_ref[:, cols] + jnp.dot(mixed_ref[...], wout_ref[:, cols], preferred_element_type=F32)

    def attn_unit(attn_w, j, g):
        lo_bias = jnp.where(col >= row, 0.0, NEG)
        hi_bias = jnp.where(col <= row, 0.0, NEG)
        if j == 0:
            lo_bias = lo_bias + jnp.where(first_a, NEG, 0.0)
        if j == nqb - 1:
            hi_bias = hi_bias + jnp.where(last_a, NEG, 0.0)
        rows = slice(j * WINDOW, (j + 1) * WINDOW)

        def diag(base):
            lo = [kv_block(b, slice(base, base + LANES)) for b in (j - 1, j, j + 1)]
            hi = [kv_block(b, slice(base + LANES, base + 2 * LANES)) for b in (j - 1, j, j + 1)]
            return jnp.concatenate(lo + hi, axis=0)

        kd = diag(g * 4 * LANES)
        vd = jnp.concatenate([diag(g * 4 * LANES + 2 * LANES), head_ones], axis=1)
        qa = q_ref[rows, g * 2 * LANES:g * 2 * LANES + LANES]
        qb = q_ref[rows, g * 2 * LANES + LANES:(g + 1) * 2 * LANES]
        qq = jnp.concatenate([qa, qb], axis=0)
        s = lax.dot_general(qq, kd, (((1,), (1,)), ((), ())), preferred_element_type=F32)
        probs = []
        sinks = []
        for pair in range(2):
            pr = slice(pair * WINDOW, (pair + 1) * WINDOW)
            row_p = []
            row_sink = []
            for half in range(2):
                off = half * 3 * WINDOW
                sink = sink_ref[4 * g + 2 * pair + half] * LOG2E
                s0 = s[pr, off:off + WINDOW] + lo_bias
                s1 = s[pr, off + WINDOW:off + 2 * WINDOW]
                s2 = s[pr, off + 2 * WINDOW:off + 3 * WINDOW] + hi_bias
                m = jnp.max(jnp.maximum(jnp.maximum(s0, s1), s2), axis=-1, keepdims=True)
                m = jnp.maximum(m, sink)
                row_p += [jnp.exp2(s0 - m).astype(BF16), jnp.exp2(s1 - m).astype(BF16),
                          jnp.exp2(s2 - m).astype(BF16)]
                row_sink.append(jnp.exp2(sink - m))
            probs.append(jnp.concatenate(row_p, axis=1))
            sinks.append(jnp.where(low_half, row_sink[0], row_sink[1]))
        pp = jnp.concatenate(probs, axis=0)
        o = jnp.dot(pp, vd, preferred_element_type=F32)
        for pair in range(2):
            pr = slice(pair * WINDOW, (pair + 1) * WINDOW)
            denom = o[pr, LANES:] + sinks[pair]
            attn_w[rows, (2 * g + pair) * LANES:(2 * g + pair + 1) * LANES] = o[pr, :LANES] / denom

    def step(cur, old):
        mixed_ref[:, 0:ATTN_WIDTH] = (
            _rms(attn_ref[old], ga_ref[...]) * sga_ref[...].astype(F32)).astype(BF16)
        mixed_ref[:, ATTN_WIDTH:] = cm_ref[...]
        units = [(j, g) for j in range(nqb) for g in range(N_KV_HEADS)]
        n_out = D_MODEL // OUT_CHUNK
        for u, (j, g) in enumerate(units):
            for c in range(u * n_out // len(units), (u + 1) * n_out // len(units)):
                out_chunk(c)
            attn_unit(attn_ref.at[cur], j, g)
        if final:
            o_ref[...] = _rms(o_ref[...], fin_ref[...])

    for parity in range(2):
        @pl.when(s_id % 2 == parity)
        def _():
            step(parity, 1 - parity)


def _mixer(x, q, kvd, sga, cm, sink, g_attn, w_out, final_norm, layer, seq, *, final):
    n_tok = x.shape[0]
    t = MIX_TILE
    n = n_tok // t
    tps = seq // t
    nblk = t // WINDOW
    att = lambda s: jnp.minimum(s, n - 1)
    gat = lambda s: jnp.maximum(s - 1, 0)
    tok = lambda stage, width: pl.BlockSpec((t, width), lambda s, *_: (stage(s), 0))
    grid_spec = pltpu.PrefetchScalarGridSpec(
        num_scalar_prefetch=1,
        grid=(n + 1,),
        in_specs=[
            tok(gat, D_MODEL),
            tok(att, ATTN_WIDTH),
            pl.BlockSpec((WINDOW, KVD_WIDTH), lambda s, *_: (jnp.maximum(att(s) * nblk - 1, 0), 0)),
            tok(att, KVD_WIDTH),
            pl.BlockSpec((WINDOW, KVD_WIDTH),
                         lambda s, *_: (jnp.minimum((att(s) + 1) * nblk, n_tok // WINDOW - 1), 0)),
            tok(gat, ATTN_WIDTH),
            tok(gat, CM_WIDTH),
            pl.BlockSpec((None, 1, ATTN_WIDTH), lambda s, *_: (layer, 0, 0)),
            pl.BlockSpec((None, D_MODEL, W_OUT_PITCH), lambda s, *_: (layer, 0, 0),
                         pipeline_mode=pl.Buffered(1)),
            pl.BlockSpec((1, D_MODEL), lambda s, *_: (0, 0)),
        ],
        out_specs=tok(gat, D_MODEL),
        scratch_shapes=[pltpu.VMEM((2, t, ATTN_WIDTH), F32), pltpu.VMEM((t, D_MODEL), BF16)],
    )
    return pl.pallas_call(
        functools.partial(_mixer_kernel, final=final, tiles_per_seq=tps),
        grid_spec=grid_spec,
        out_shape=jax.ShapeDtypeStruct((n_tok, D_MODEL), F32),
        compiler_params=pltpu.CompilerParams(
            dimension_semantics=("arbitrary",), vmem_limit_bytes=VMEM_LIMIT_BYTES),
        name="mixer",
    )(sink, x, q, kvd, kvd, kvd, sga, cm, g_attn, w_out, final_norm)


def _trunk(x, mem, cos, sin, norm_in, w_in, attn_sink, conv_w, norm_mem, w_mem_kv,
           g_attn, g_conv, g_mem, w_out, final_norm):
    depth = w_in.shape[0]
    b, seq, _ = x.shape
    cos, sin = cos[:seq], sin[:seq]
    x = x.reshape(b * seq, D_MODEL)
    for l in range(depth):
        mkv = _mem_kv(mem, norm_mem, w_mem_kv, l)
        q, kvd, sga, cm = _in_proj(x, mkv, norm_in, w_in, conv_w, g_conv, g_mem, l, cos, sin, seq)
        x = _mixer(x, q, kvd, sga, cm, attn_sink[l], g_attn, w_out, final_norm, l, seq, final=(l == depth - 1))
    return x.reshape(b, seq, D_MODEL)


def kernel(x_prompt, x_sample, mem_prompt, mem_sample, norm_in, w_in, attn_sink, conv_w, norm_mem, w_mem_kv,
           g_attn, g_conv, g_mem, w_out, final_norm):
    depth = w_in.shape[0]
    cos, sin = _rope_tables(max(x_prompt.shape[1], x_sample.shape[1]))
    weights = (norm_in.reshape(depth, 1, D_MODEL), w_in.astype(BF16), attn_sink, conv_w,
               norm_mem.reshape(depth, 1, D_MODEL), w_mem_kv.astype(BF16),
               g_attn.reshape(depth, 1, ATTN_WIDTH), g_conv.reshape(depth, 1, CONV_WIDTH),
               g_mem.reshape(depth, 1, X_WIDTH),
               jnp.pad(w_out.astype(BF16), ((0, 0), (0, 0), (0, W_OUT_PITCH - D_MODEL))),
               final_norm.reshape(1, D_MODEL))
    y_prompt = _trunk(x_prompt, mem_prompt, cos, sin, *weights)
    y_sample = _trunk(x_sample, mem_sample, cos, sin, *weights)
    return (y_prompt, y_sample)
```

```python
import functools

import jax
import jax.numpy as jnp
from jax import lax
from jax.experimental import pallas as pl
from jax.experimental.pallas import tpu as pltpu

D_MODEL = 2048
HEAD_DIM = 64
N_Q_HEADS = 16
N_KV_HEADS = 4
ATTN_WIDTH = N_Q_HEADS * HEAD_DIM
KV_WIDTH = N_KV_HEADS * HEAD_DIM
WINDOW = 128
CONV_WIDTH = 512
CONV_K = 3
N_MEM = 256
N_X_HEADS = 4
X_WIDTH = 512
X_HEAD_DIM = X_WIDTH // N_X_HEADS
ROPE_THETA = 10000.0
EPS = 1e-6
D_IN = 5632
CHUNK = 512

LANES = 128
F32_ROWS = 8
VMEM_LIMIT_BYTES = 56 * 1024 * 1024
LOG2E = 1.4426950408889634
NEG = -1e30
KVD_WIDTH = N_KV_HEADS * 4 * LANES
CM_WIDTH = CONV_WIDTH + X_WIDTH
ATT_WIDTH = KVD_WIDTH + ATTN_WIDTH
GATE_WIDTH = ATTN_WIDTH + CM_WIDTH

ROPE_TILE = 2048
PROJ_TILE = 512
MIX_TILE = 256
OUT_CHUNK = 256
W_OUT_PITCH = D_MODEL + LANES

BF16 = jnp.bfloat16
F32 = jnp.float32


def _rms(x, g):
    ms = jnp.mean(x * x, axis=-1, keepdims=True)
    return (x * lax.rsqrt(ms + EPS)) * g


def _silu(x):
    return x / (1.0 + jnp.exp(-x))


def _rope_table_kernel(inv_ref, cos_ref, sin_ref, base_cos, base_sin):
    i = pl.program_id(0)
    nblk = cos_ref.shape[0] // WINDOW
    inv = inv_ref[...]

    @pl.when(i == 0)
    def _():
        ang = lax.broadcasted_iota(jnp.int32, (WINDOW, LANES), 0).astype(F32) * inv
        base_cos[...] = jnp.cos(ang)
        base_sin[...] = jnp.sin(ang)

    first_row = (lax.broadcasted_iota(jnp.int32, (nblk, LANES), 0) + i * nblk) * WINDOW
    ang0 = first_row.astype(F32) * inv
    cos0 = jnp.cos(ang0)
    sin0 = jnp.sin(ang0)
    lane = lax.broadcasted_iota(jnp.int32, (WINDOW, LANES), 1)
    first_half = (lane % HEAD_DIM) < HEAD_DIM // 2
    cb = base_cos[...]
    sb = base_sin[...]
    for b in range(nblk):
        c0 = cos0[b:b + 1, :]
        s0 = sin0[b:b + 1, :]
        cos_ref[b * WINDOW:(b + 1) * WINDOW, :] = c0 * cb - s0 * sb
        sn = s0 * cb + c0 * sb
        sin_ref[b * WINDOW:(b + 1) * WINDOW, :] = jnp.where(first_half, -sn, sn)


def _rope_tables(seq):
    inv_freq = ROPE_THETA ** (-jnp.arange(0, HEAD_DIM, 2, dtype=jnp.float32) / HEAD_DIM)
    inv = jnp.tile(inv_freq, LANES // (HEAD_DIM // 2)).reshape(1, LANES)
    tile = min(ROPE_TILE, seq)
    return pl.pallas_call(
        _rope_table_kernel,
        grid=(seq // tile,),
        in_specs=[pl.BlockSpec((1, LANES), lambda i: (0, 0))],
        out_specs=[pl.BlockSpec((tile, LANES), lambda i: (i, 0))] * 2,
        out_shape=[jax.ShapeDtypeStruct((seq, LANES), F32)] * 2,
        scratch_shapes=[pltpu.VMEM((WINDOW, LANES), F32)] * 2,
        compiler_params=pltpu.CompilerParams(dimension_semantics=("arbitrary",)),
        name="rope_tables",
    )(inv)


def _mem_kv_kernel(memp_ref, mems_ref, g_ref, w_ref, o_ref, *, n_prompt):
    mem = jnp.where(pl.program_id(1) < n_prompt, memp_ref[0], mems_ref[0])
    h = _rms(mem, g_ref[...]).astype(BF16)
    o_ref[0] = jnp.dot(h, w_ref[...], preferred_element_type=F32).astype(BF16)


def _mem_kv(mem_prompt, mem_sample, g, w):
    depth = w.shape[0]
    n_p, n_s = mem_prompt.shape[0], mem_sample.shape[0]
    return pl.pallas_call(
        functools.partial(_mem_kv_kernel, n_prompt=n_p),
        grid=(depth, n_p + n_s),
        in_specs=[pl.BlockSpec((1, N_MEM, D_MODEL), lambda l, i: (jnp.minimum(i, n_p - 1), 0, 0)),
                  pl.BlockSpec((1, N_MEM, D_MODEL), lambda l, i: (jnp.maximum(i - n_p, 0), 0, 0)),
                  pl.BlockSpec((None, 1, D_MODEL), lambda l, i: (l, 0, 0)),
                  pl.BlockSpec((None, D_MODEL, 2 * X_WIDTH), lambda l, i: (l, 0, 0))],
        out_specs=pl.BlockSpec((None, 1, N_MEM, 2 * X_WIDTH), lambda l, i: (l, i, 0, 0)),
        out_shape=jax.ShapeDtypeStruct((depth, n_p + n_s, N_MEM, 2 * X_WIDTH), BF16),
        name="mem_kv",
    )(mem_prompt, mem_sample, g, w)


def _swap_halves(x):
    lane = lax.broadcasted_iota(jnp.int32, x.shape, 1)
    first = (lane % HEAD_DIM) < HEAD_DIM // 2
    return jnp.where(first, pltpu.roll(x, LANES - HEAD_DIM // 2, 1), pltpu.roll(x, HEAD_DIM // 2, 1))


def _in_proj_kernel(x_ref, xp_ref, xn_ref, g_ref, w_ref, cos_ref, sin_ref, mkv_ref, convw_ref, gc_ref, gm_ref,
                    att_ref, gate_ref, *, tiles_per_seq):
    t = PROJ_TILE
    i = pl.program_id(0)
    first = i % tiles_per_seq == 0
    last = i % tiles_per_seq == tiles_per_seq - 1
    g = g_ref[...]
    h = _rms(x_ref[...], g).astype(BF16)
    halo = jnp.concatenate([xp_ref[...], xn_ref[...]], axis=0)
    h_ext = jnp.concatenate([h, _rms(halo, g).astype(BF16)], axis=0)
    cos = cos_ref[...]
    sin = sin_ref[...]
    low_half = lax.broadcasted_iota(jnp.int32, cos.shape, 1) < HEAD_DIM

    def proj(c, lhs=h):
        return jnp.dot(lhs, w_ref[:, c * CHUNK:(c + 1) * CHUNK], preferred_element_type=F32)

    def rope(xc):
        return xc * cos + _swap_halves(xc) * sin

    mq = proj(9).astype(BF16)
    xo = []
    for hd in range(N_X_HEADS):
        mk = mkv_ref[:, hd * X_HEAD_DIM:(hd + 1) * X_HEAD_DIM]
        mv = mkv_ref[:, X_WIDTH + hd * X_HEAD_DIM:X_WIDTH + (hd + 1) * X_HEAD_DIM]
        sm = lax.dot_general(mq[:, hd * X_HEAD_DIM:(hd + 1) * X_HEAD_DIM], mk, (((1,), (1,)), ((), ())),
                             preferred_element_type=F32)
        sm = sm * (X_HEAD_DIM ** -0.5)
        m = jnp.max(sm, axis=-1, keepdims=True)
        p = jnp.exp(sm - m)
        l = jnp.sum(p, axis=-1, keepdims=True)
        xo.append(jnp.dot(p.astype(BF16), mv, preferred_element_type=F32) * (1.0 / l))
    xo = jnp.concatenate(xo, axis=1)
    gate_ref[:, ATTN_WIDTH + CONV_WIDTH:] = (_rms(xo, gm_ref[...]) * _silu(proj(10))).astype(BF16)

    zc = proj(6, h_ext) * proj(7, h_ext)
    z = zc[:t]
    z_before = jnp.where(first, 0.0, zc[t + F32_ROWS - 1:t + F32_ROWS])
    z_after = jnp.where(last, 0.0, zc[t + F32_ROWS:t + F32_ROWS + 1])
    trow = lax.broadcasted_iota(jnp.int32, z.shape, 0)
    z_prev = jnp.where(trow == 0, z_before, pltpu.roll(z, 1, 0))
    z_next = jnp.where(trow == t - 1, z_after, pltpu.roll(z, t - 1, 0))
    cw = convw_ref[...]
    conv = proj(5) * (cw[0:1] * z_prev + cw[1:2] * z + cw[2:3] * z_next)
    gate_ref[:, ATTN_WIDTH:ATTN_WIDTH + CONV_WIDTH] = (_rms(conv, gc_ref[...]) * _silu(proj(8))).astype(BF16)

    pc = proj(2)
    for s in range(CHUNK // LANES):
        xc = pc[:, s * LANES:(s + 1) * LANES]
        is_v = s >= KV_WIDTH // LANES
        c = s - KV_WIDTH // LANES if is_v else s
        if not is_v:
            xc = rope(xc)
        sw = pltpu.roll(xc, HEAD_DIM, 1)
        for head, (lo_src, hi_src) in ((2 * c, (xc, sw)), (2 * c + 1, (sw, xc))):
            base = head * 4 * LANES + (2 * LANES if is_v else 0)
            att_ref[:, base:base + LANES] = jnp.where(low_half, lo_src, 0.0).astype(BF16)
            att_ref[:, base + LANES:base + 2 * LANES] = jnp.where(low_half, 0.0, hi_src).astype(BF16)
    for c in range(2):
        pc = proj(c)
        for s in range(CHUNK // LANES):
            xc = pc[:, s * LANES:(s + 1) * LANES]
            att_ref[:, KVD_WIDTH + c * CHUNK + s * LANES:KVD_WIDTH + c * CHUNK + (s + 1) * LANES] = (
                rope(xc) * (HEAD_DIM ** -0.5 * LOG2E)).astype(BF16)
    for dst, c in enumerate((3, 4)):
        gate_ref[:, dst * CHUNK:(dst + 1) * CHUNK] = _silu(proj(c)).astype(BF16)


def _in_proj(x, mkv, mem_base, g, w, conv_w, g_conv, g_mem, layer, cos, sin, seq):
    n_tok = x.shape[0]
    t = PROJ_TILE
    tps = seq // t
    nh = t // F32_ROWS
    tok = lambda width: pl.BlockSpec((t, width), lambda i: (i, 0))
    per_layer = lambda r, c: pl.BlockSpec((None, r, c), lambda i: (layer, 0, 0))
    return pl.pallas_call(
        functools.partial(_in_proj_kernel, tiles_per_seq=tps),
        grid=(n_tok // t,),
        in_specs=[tok(D_MODEL),
                  pl.BlockSpec((F32_ROWS, D_MODEL), lambda i: (jnp.maximum(i * nh - 1, 0), 0)),
                  pl.BlockSpec((F32_ROWS, D_MODEL), lambda i: (jnp.minimum((i + 1) * nh, n_tok // F32_ROWS - 1), 0)),
                  per_layer(1, D_MODEL),
                  pl.BlockSpec((None, D_MODEL, D_IN), lambda i: (layer, 0, 0), pipeline_mode=pl.Buffered(1)),
                  pl.BlockSpec((t, LANES), lambda i: (i % tps, 0)),
                  pl.BlockSpec((t, LANES), lambda i: (i % tps, 0)),
                  pl.BlockSpec((None, None, N_MEM, 2 * X_WIDTH), lambda i: (layer, mem_base + i // tps, 0, 0)),
                  per_layer(CONV_K, CONV_WIDTH), per_layer(1, CONV_WIDTH), per_layer(1, X_WIDTH)],
        out_specs=[tok(ATT_WIDTH), tok(GATE_WIDTH)],
        out_shape=[jax.ShapeDtypeStruct((n_tok, ATT_WIDTH), BF16),
                   jax.ShapeDtypeStruct((n_tok, GATE_WIDTH), BF16)],
        compiler_params=pltpu.CompilerParams(
            dimension_semantics=("parallel",), vmem_limit_bytes=VMEM_LIMIT_BYTES),
        name="in_proj",
    )(x, x, x, g, w, cos, sin, mkv, conv_w, g_conv, g_mem)


def _mixer_kernel(sink_ref, x_ref, kvp_ref, att_ref, kvn_ref, gate_ref, ga_ref, wout_ref, fin_ref,
                  o_ref, attn_ref, mixed_ref, *, final, tiles_per_seq):
    t = MIX_TILE
    nqb = t // WINDOW
    s_id = pl.program_id(0)
    n = pl.num_programs(0) - 1
    tile_a = jnp.minimum(s_id, n - 1)
    first_a = tile_a % tiles_per_seq == 0
    last_a = tile_a % tiles_per_seq == tiles_per_seq - 1

    @pl.when(s_id == 0)
    def _():
        attn_ref[...] = jnp.zeros(attn_ref.shape, F32)

    row = lax.broadcasted_iota(jnp.int32, (WINDOW, WINDOW), 0)
    col = lax.broadcasted_iota(jnp.int32, (WINDOW, WINDOW), 1)
    low_half = lax.broadcasted_iota(jnp.int32, (WINDOW, LANES), 1) < HEAD_DIM
    ones_lo = jnp.where(low_half, 1.0, 0.0).astype(BF16)
    ones_hi = jnp.where(low_half, 0.0, 1.0).astype(BF16)
    head_ones = jnp.concatenate([ones_lo] * 3 + [ones_hi] * 3, axis=0)

    def kv_block(blk, cols):
        if blk < 0:
            return kvp_ref[:, cols]
        if blk >= nqb:
            return kvn_ref[:, cols]
        return att_ref[blk * WINDOW:(blk + 1) * WINDOW, cols]

    def out_chunk(c):
        cols = slice(c * OUT_CHUNK, (c + 1) * OUT_CHUNK)
        o_ref[:, cols] = x_ref[:, cols] + jnp.dot(mixed_ref[...], wout_ref[:, cols], preferred_element_type=F32)

    def attn_unit(attn_w, j, g):
        lo_bias = jnp.where(col >= row, 0.0, NEG)
        hi_bias = jnp.where(col <= row, 0.0, NEG)
        if j == 0:
            lo_bias = lo_bias + jnp.where(first_a, NEG, 0.0)
        if j == nqb - 1:
            hi_bias = hi_bias + jnp.where(last_a, NEG, 0.0)
        rows = slice(j * WINDOW, (j + 1) * WINDOW)

        def diag(base):
            lo = [kv_block(b, slice(base, base + LANES)) for b in (j - 1, j, j + 1)]
            hi = [kv_block(b, slice(base + LANES, base + 2 * LANES)) for b in (j - 1, j, j + 1)]
            return jnp.concatenate(lo + hi, axis=0)

        kd = diag(g * 4 * LANES)
        vd = jnp.concatenate([diag(g * 4 * LANES + 2 * LANES), head_ones], axis=1)
        qa = att_ref[rows, KVD_WIDTH + g * 2 * LANES:KVD_WIDTH + g * 2 * LANES + LANES]
        qb = att_ref[rows, KVD_WIDTH + g * 2 * LANES + LANES:KVD_WIDTH + (g + 1) * 2 * LANES]
        qq = jnp.concatenate([qa, qb], axis=0)
        s = lax.dot_general(qq, kd, (((1,), (1,)), ((), ())), preferred_element_type=F32)
        probs = []
        sinks = []
        for pair in range(2):
            pr = slice(pair * WINDOW, (pair + 1) * WINDOW)
            row_p = []
            row_sink = []
            for half in range(2):
                off = half * 3 * WINDOW
                sink = sink_ref[4 * g + 2 * pair + half] * LOG2E
                s0 = s[pr, off:off + WINDOW] + lo_bias
                s1 = s[pr, off + WINDOW:off + 2 * WINDOW]
                s2 = s[pr, off + 2 * WINDOW:off + 3 * WINDOW] + hi_bias
                m = jnp.max(jnp.maximum(jnp.maximum(s0, s1), s2), axis=-1, keepdims=True)
                m = jnp.maximum(m, sink)
                row_p += [jnp.exp2(s0 - m).astype(BF16), jnp.exp2(s1 - m).astype(BF16),
                          jnp.exp2(s2 - m).astype(BF16)]
                row_sink.append(jnp.exp2(sink - m))
            probs.append(jnp.concatenate(row_p, axis=1))
            sinks.append(jnp.where(low_half, row_sink[0], row_sink[1]))
        pp = jnp.concatenate(probs, axis=0)
        o = jnp.dot(pp, vd, preferred_element_type=F32)
        for pair in range(2):
            pr = slice(pair * WINDOW, (pair + 1) * WINDOW)
            denom = o[pr, LANES:] + sinks[pair]
            attn_w[rows, (2 * g + pair) * LANES:(2 * g + pair + 1) * LANES] = o[pr, :LANES] / denom

    def step(cur, old):
        mixed_ref[:, 0:ATTN_WIDTH] = (
            _rms(attn_ref[old], ga_ref[...]) * gate_ref[:, 0:ATTN_WIDTH].astype(F32)).astype(BF16)
        mixed_ref[:, ATTN_WIDTH:] = gate_ref[:, ATTN_WIDTH:]
        units = [(j, g) for j in range(nqb) for g in range(N_KV_HEADS)]
        n_out = D_MODEL // OUT_CHUNK
        for u, (j, g) in enumerate(units):
            for c in range(u * n_out // len(units), (u + 1) * n_out // len(units)):
                out_chunk(c)
            attn_unit(attn_ref.at[cur], j, g)
        if final:
            o_ref[...] = _rms(o_ref[...], fin_ref[...])

    for parity in range(2):
        @pl.when(s_id % 2 == parity)
        def _():
            step(parity, 1 - parity)


def _mixer(x, att_stream, gate_stream, sink, g_attn, w_out, final_norm, layer, seq, *, final):
    n_tok = x.shape[0]
    t = MIX_TILE
    n = n_tok // t
    tps = seq // t
    nblk = t // WINDOW
    att = lambda s: jnp.minimum(s, n - 1)
    gat = lambda s: jnp.maximum(s - 1, 0)
    tok = lambda stage, width: pl.BlockSpec((t, width), lambda s, *_: (stage(s), 0))
    grid_spec = pltpu.PrefetchScalarGridSpec(
        num_scalar_prefetch=1,
        grid=(n + 1,),
        in_specs=[
            tok(gat, D_MODEL),
            pl.BlockSpec((WINDOW, KVD_WIDTH), lambda s, *_: (jnp.maximum(att(s) * nblk - 1, 0), 0)),
            tok(att, ATT_WIDTH),
            pl.BlockSpec((WINDOW, KVD_WIDTH),
                         lambda s, *_: (jnp.minimum((att(s) + 1) * nblk, n_tok // WINDOW - 1), 0)),
            tok(gat, GATE_WIDTH),
            pl.BlockSpec((None, 1, ATTN_WIDTH), lambda s, *_: (layer, 0, 0)),
            pl.BlockSpec((None, D_MODEL, W_OUT_PITCH), lambda s, *_: (layer, 0, 0),
                         pipeline_mode=pl.Buffered(1)),
            pl.BlockSpec((1, D_MODEL), lambda s, *_: (0, 0)),
        ],
        out_specs=tok(gat, D_MODEL),
        scratch_shapes=[pltpu.VMEM((2, t, ATTN_WIDTH), F32), pltpu.VMEM((t, D_MODEL), BF16)],
    )
    return pl.pallas_call(
        functools.partial(_mixer_kernel, final=final, tiles_per_seq=tps),
        grid_spec=grid_spec,
        out_shape=jax.ShapeDtypeStruct((n_tok, D_MODEL), F32),
        compiler_params=pltpu.CompilerParams(
            dimension_semantics=("arbitrary",), vmem_limit_bytes=VMEM_LIMIT_BYTES),
        name="mixer",
    )(sink, x, att_stream, att_stream, att_stream, gate_stream, g_attn, w_out, final_norm)


def _trunk(x, mkv, mem_base, cos, sin, norm_in, w_in, attn_sink, conv_w, g_attn, g_conv, g_mem, w_out, final_norm):
    depth = w_in.shape[0]
    b, seq, _ = x.shape
    x = x.reshape(b * seq, D_MODEL)
    for l in range(depth):
        att, gate = _in_proj(x, mkv, mem_base, norm_in, w_in, conv_w, g_conv, g_mem, l, cos, sin, seq)
        x = _mixer(x, att, gate, attn_sink[l], g_attn, w_out, final_norm, l, seq, final=(l == depth - 1))
    return x.reshape(b, seq, D_MODEL)


def kernel(x_prompt, x_sample, mem_prompt, mem_sample, norm_in, w_in, attn_sink, conv_w, norm_mem, w_mem_kv,
           g_attn, g_conv, g_mem, w_out, final_norm):
    depth = w_in.shape[0]
    cos, sin = _rope_tables(max(x_prompt.shape[1], x_sample.shape[1]))
    mkv = _mem_kv(mem_prompt, mem_sample, norm_mem.reshape(depth, 1, D_MODEL), w_mem_kv.astype(BF16))
    weights = (norm_in.reshape(depth, 1, D_MODEL), w_in.astype(BF16), attn_sink, conv_w,
               g_attn.reshape(depth, 1, ATTN_WIDTH), g_conv.reshape(depth, 1, CONV_WIDTH),
               g_mem.reshape(depth, 1, X_WIDTH),
               jnp.pad(w_out.astype(BF16), ((0, 0), (0, 0), (0, W_OUT_PITCH - D_MODEL))),
               final_norm.reshape(1, D_MODEL))
    y_prompt = _trunk(x_prompt, mkv, 0, cos, sin, *weights)
    y_sample = _trunk(x_sample, mkv, mem_prompt.shape[0], cos, sin, *weights)
    return (y_prompt, y_sample)
```

```python
import functools

import jax
import jax.numpy as jnp
from jax import lax
from jax.experimental import pallas as pl
from jax.experimental.pallas import tpu as pltpu

D_MODEL = 2048
HEAD_DIM = 64
N_Q_HEADS = 16
N_KV_HEADS = 4
ATTN_WIDTH = N_Q_HEADS * HEAD_DIM
KV_WIDTH = N_KV_HEADS * HEAD_DIM
WINDOW = 128
CONV_WIDTH = 512
CONV_K = 3
N_MEM = 256
N_X_HEADS = 4
X_WIDTH = 512
X_HEAD_DIM = X_WIDTH // N_X_HEADS
ROPE_THETA = 10000.0
EPS = 1e-6
D_IN = 5632
CHUNK = 512

LANES = 128
F32_ROWS = 8
VMEM_LIMIT_BYTES = 56 * 1024 * 1024
LOG2E = 1.4426950408889634
NEG = -1e30
KVD_WIDTH = N_KV_HEADS * 4 * LANES
CM_WIDTH = CONV_WIDTH + X_WIDTH
ATT_WIDTH = KVD_WIDTH + ATTN_WIDTH
GATE_WIDTH = ATTN_WIDTH + CM_WIDTH

ROPE_TILE = 2048
PROJ_TILE = 512
MIX_TILE = 512
SUB_TILE = 256
OUT_CHUNK = 256
W_OUT_PITCH = D_MODEL + LANES

BF16 = jnp.bfloat16
F32 = jnp.float32


def _rms(x, g):
    ms = jnp.mean(x * x, axis=-1, keepdims=True)
    return (x * lax.rsqrt(ms + EPS)) * g


def _silu(x):
    return x / (1.0 + jnp.exp(-x))


def _rope_table_kernel(inv_ref, cos_ref, sin_ref, base_cos, base_sin):
    i = pl.program_id(0)
    nblk = cos_ref.shape[0] // WINDOW
    inv = inv_ref[...]

    @pl.when(i == 0)
    def _():
        ang = lax.broadcasted_iota(jnp.int32, (WINDOW, LANES), 0).astype(F32) * inv
        base_cos[...] = jnp.cos(ang)
        base_sin[...] = jnp.sin(ang)

    first_row = (lax.broadcasted_iota(jnp.int32, (nblk, LANES), 0) + i * nblk) * WINDOW
    ang0 = first_row.astype(F32) * inv
    cos0 = jnp.cos(ang0)
    sin0 = jnp.sin(ang0)
    lane = lax.broadcasted_iota(jnp.int32, (WINDOW, LANES), 1)
    first_half = (lane % HEAD_DIM) < HEAD_DIM // 2
    cb = base_cos[...]
    sb = base_sin[...]
    for b in range(nblk):
        c0 = cos0[b:b + 1, :]
        s0 = sin0[b:b + 1, :]
        cos_ref[b * WINDOW:(b + 1) * WINDOW, :] = c0 * cb - s0 * sb
        sn = s0 * cb + c0 * sb
        sin_ref[b * WINDOW:(b + 1) * WINDOW, :] = jnp.where(first_half, -sn, sn)


def _rope_tables(seq):
    inv_freq = ROPE_THETA ** (-jnp.arange(0, HEAD_DIM, 2, dtype=jnp.float32) / HEAD_DIM)
    inv = jnp.tile(inv_freq, LANES // (HEAD_DIM // 2)).reshape(1, LANES)
    tile = min(ROPE_TILE, seq)
    return pl.pallas_call(
        _rope_table_kernel,
        grid=(seq // tile,),
        in_specs=[pl.BlockSpec((1, LANES), lambda i: (0, 0))],
        out_specs=[pl.BlockSpec((tile, LANES), lambda i: (i, 0))] * 2,
        out_shape=[jax.ShapeDtypeStruct((seq, LANES), F32)] * 2,
        scratch_shapes=[pltpu.VMEM((WINDOW, LANES), F32)] * 2,
        compiler_params=pltpu.CompilerParams(dimension_semantics=("arbitrary",)),
        name="rope_tables",
    )(inv)


def _mem_kv_kernel(memp_ref, mems_ref, g_ref, w_ref, o_ref, *, n_prompt):
    mem = jnp.where(pl.program_id(1) < n_prompt, memp_ref[0], mems_ref[0])
    h = _rms(mem, g_ref[...]).astype(BF16)
    o_ref[0] = jnp.dot(h, w_ref[...], preferred_element_type=F32).astype(BF16)


def _mem_kv(mem_prompt, mem_sample, g, w):
    depth = w.shape[0]
    n_p, n_s = mem_prompt.shape[0], mem_sample.shape[0]
    return pl.pallas_call(
        functools.partial(_mem_kv_kernel, n_prompt=n_p),
        grid=(depth, n_p + n_s),
        in_specs=[pl.BlockSpec((1, N_MEM, D_MODEL), lambda l, i: (jnp.minimum(i, n_p - 1), 0, 0)),
                  pl.BlockSpec((1, N_MEM, D_MODEL), lambda l, i: (jnp.maximum(i - n_p, 0), 0, 0)),
                  pl.BlockSpec((None, 1, D_MODEL), lambda l, i: (l, 0, 0)),
                  pl.BlockSpec((None, D_MODEL, 2 * X_WIDTH), lambda l, i: (l, 0, 0))],
        out_specs=pl.BlockSpec((None, 1, N_MEM, 2 * X_WIDTH), lambda l, i: (l, i, 0, 0)),
        out_shape=jax.ShapeDtypeStruct((depth, n_p + n_s, N_MEM, 2 * X_WIDTH), BF16),
        name="mem_kv",
    )(mem_prompt, mem_sample, g, w)


def _swap_halves(x):
    lane = lax.broadcasted_iota(jnp.int32, x.shape, 1)
    first = (lane % HEAD_DIM) < HEAD_DIM // 2
    return jnp.where(first, pltpu.roll(x, LANES - HEAD_DIM // 2, 1), pltpu.roll(x, HEAD_DIM // 2, 1))


def _in_proj_kernel(x_ref, xp_ref, xn_ref, g_ref, w_ref, cos_ref, sin_ref, mkv_ref, convw_ref, gc_ref, gm_ref,
                    att_ref, gate_ref, *, tiles_per_seq):
    t = PROJ_TILE
    i = pl.program_id(0)
    first = i % tiles_per_seq == 0
    last = i % tiles_per_seq == tiles_per_seq - 1
    g = g_ref[...]
    h = _rms(x_ref[...], g).astype(BF16)
    halo = jnp.concatenate([xp_ref[...], xn_ref[...]], axis=0)
    h_ext = jnp.concatenate([h, _rms(halo, g).astype(BF16)], axis=0)
    cos = cos_ref[...]
    sin = sin_ref[...]
    low_half = lax.broadcasted_iota(jnp.int32, cos.shape, 1) < HEAD_DIM

    def proj(c, lhs=h):
        return jnp.dot(lhs, w_ref[:, c * CHUNK:(c + 1) * CHUNK], preferred_element_type=F32)

    def rope(xc):
        return xc * cos + _swap_halves(xc) * sin

    mq = proj(9).astype(BF16)
    xo = []
    for hd in range(N_X_HEADS):
        mk = mkv_ref[:, hd * X_HEAD_DIM:(hd + 1) * X_HEAD_DIM]
        mv = mkv_ref[:, X_WIDTH + hd * X_HEAD_DIM:X_WIDTH + (hd + 1) * X_HEAD_DIM]
        sm = lax.dot_general(mq[:, hd * X_HEAD_DIM:(hd + 1) * X_HEAD_DIM], mk, (((1,), (1,)), ((), ())),
                             preferred_element_type=F32)
        sm = sm * (X_HEAD_DIM ** -0.5)
        m = jnp.max(sm, axis=-1, keepdims=True)
        p = jnp.exp(sm - m)
        l = jnp.sum(p, axis=-1, keepdims=True)
        xo.append(jnp.dot(p.astype(BF16), mv, preferred_element_type=F32) * (1.0 / l))
    xo = jnp.concatenate(xo, axis=1)
    gate_ref[:, ATTN_WIDTH + CONV_WIDTH:] = (_rms(xo, gm_ref[...]) * _silu(proj(10))).astype(BF16)

    zc = proj(6, h_ext) * proj(7, h_ext)
    z = zc[:t]
    z_before = jnp.where(first, 0.0, zc[t + F32_ROWS - 1:t + F32_ROWS])
    z_after = jnp.where(last, 0.0, zc[t + F32_ROWS:t + F32_ROWS + 1])
    trow = lax.broadcasted_iota(jnp.int32, z.shape, 0)
    z_prev = jnp.where(trow == 0, z_before, pltpu.roll(z, 1, 0))
    z_next = jnp.where(trow == t - 1, z_after, pltpu.roll(z, t - 1, 0))
    cw = convw_ref[...]
    conv = proj(5) * (cw[0:1] * z_prev + cw[1:2] * z + cw[2:3] * z_next)
    gate_ref[:, ATTN_WIDTH:ATTN_WIDTH + CONV_WIDTH] = (_rms(conv, gc_ref[...]) * _silu(proj(8))).astype(BF16)

    pc = proj(2)
    for s in range(CHUNK // LANES):
        xc = pc[:, s * LANES:(s + 1) * LANES]
        is_v = s >= KV_WIDTH // LANES
        c = s - KV_WIDTH // LANES if is_v else s
        if not is_v:
            xc = rope(xc)
        sw = pltpu.roll(xc, HEAD_DIM, 1)
        for head, (lo_src, hi_src) in ((2 * c, (xc, sw)), (2 * c + 1, (sw, xc))):
            base = head * 4 * LANES + (2 * LANES if is_v else 0)
            att_ref[:, base:base + LANES] = jnp.where(low_half, lo_src, 0.0).astype(BF16)
            att_ref[:, base + LANES:base + 2 * LANES] = jnp.where(low_half, 0.0, hi_src).astype(BF16)
    for c in range(2):
        pc = proj(c)
        for s in range(CHUNK // LANES):
            xc = pc[:, s * LANES:(s + 1) * LANES]
            att_ref[:, KVD_WIDTH + c * CHUNK + s * LANES:KVD_WIDTH + c * CHUNK + (s + 1) * LANES] = (
                rope(xc) * (HEAD_DIM ** -0.5 * LOG2E)).astype(BF16)
    for dst, c in enumerate((3, 4)):
        gate_ref[:, dst * CHUNK:(dst + 1) * CHUNK] = _silu(proj(c)).astype(BF16)


def _in_proj(x, mkv, mem_base, g, w, conv_w, g_conv, g_mem, layer, cos, sin, seq):
    n_tok = x.shape[0]
    t = PROJ_TILE
    tps = seq // t
    nh = t // F32_ROWS
    tok = lambda width: pl.BlockSpec((t, width), lambda i: (i, 0))
    per_layer = lambda r, c: pl.BlockSpec((None, r, c), lambda i: (layer, 0, 0))
    return pl.pallas_call(
        functools.partial(_in_proj_kernel, tiles_per_seq=tps),
        grid=(n_tok // t,),
        in_specs=[tok(D_MODEL),
                  pl.BlockSpec((F32_ROWS, D_MODEL), lambda i: (jnp.maximum(i * nh - 1, 0), 0)),
                  pl.BlockSpec((F32_ROWS, D_MODEL), lambda i: (jnp.minimum((i + 1) * nh, n_tok // F32_ROWS - 1), 0)),
                  per_layer(1, D_MODEL),
                  pl.BlockSpec((None, D_MODEL, D_IN), lambda i: (layer, 0, 0), pipeline_mode=pl.Buffered(1)),
                  pl.BlockSpec((t, LANES), lambda i: (i % tps, 0)),
                  pl.BlockSpec((t, LANES), lambda i: (i % tps, 0)),
                  pl.BlockSpec((None, None, N_MEM, 2 * X_WIDTH), lambda i: (layer, mem_base + i // tps, 0, 0)),
                  per_layer(CONV_K, CONV_WIDTH), per_layer(1, CONV_WIDTH), per_layer(1, X_WIDTH)],
        out_specs=[tok(ATT_WIDTH), tok(GATE_WIDTH)],
        out_shape=[jax.ShapeDtypeStruct((n_tok, ATT_WIDTH), BF16),
                   jax.ShapeDtypeStruct((n_tok, GATE_WIDTH), BF16)],
        compiler_params=pltpu.CompilerParams(
            dimension_semantics=("parallel",), vmem_limit_bytes=VMEM_LIMIT_BYTES),
        name="in_proj",
    )(x, x, x, g, w, cos, sin, mkv, conv_w, g_conv, g_mem)


def _mixer_kernel(sink_ref, x_ref, kvp_ref, att_ref, kvn_ref, gate_ref, ga_ref, wout_ref, fin_ref,
                  o_ref, attn_ref, mixed_ref, *, final, tiles_per_seq, n):
    t = MIX_TILE
    nqb = t // WINDOW
    s_id = pl.program_id(0)
    tile_a = jnp.minimum(s_id, n - 1)
    first_a = tile_a % tiles_per_seq == 0
    last_a = tile_a % tiles_per_seq == tiles_per_seq - 1

    row = lax.broadcasted_iota(jnp.int32, (WINDOW, WINDOW), 0)
    col = lax.broadcasted_iota(jnp.int32, (WINDOW, WINDOW), 1)
    low_half = lax.broadcasted_iota(jnp.int32, (WINDOW, LANES), 1) < HEAD_DIM
    ones_lo = jnp.where(low_half, 1.0, 0.0).astype(BF16)
    ones_hi = jnp.where(low_half, 0.0, 1.0).astype(BF16)
    head_ones = jnp.concatenate([ones_lo] * 3 + [ones_hi] * 3, axis=0)

    def kv_block(blk, cols):
        if blk < 0:
            return kvp_ref[:, cols]
        if blk >= nqb:
            return kvn_ref[:, cols]
        return att_ref[blk * WINDOW:(blk + 1) * WINDOW, cols]

    def out_chunk(rows, c):
        cols = slice(c * OUT_CHUNK, (c + 1) * OUT_CHUNK)
        o_ref[rows, cols] = x_ref[rows, cols] + jnp.dot(mixed_ref[rows, :], wout_ref[:, cols],
                                                        preferred_element_type=F32)

    def attn_unit(attn_w, j, g):
        lo_bias = jnp.where(col >= row, 0.0, NEG)
        hi_bias = jnp.where(col <= row, 0.0, NEG)
        if j == 0:
            lo_bias = lo_bias + jnp.where(first_a, NEG, 0.0)
        if j == nqb - 1:
            hi_bias = hi_bias + jnp.where(last_a, NEG, 0.0)
        rows = slice(j * WINDOW, (j + 1) * WINDOW)

        def diag(base):
            lo = [kv_block(b, slice(base, base + LANES)) for b in (j - 1, j, j + 1)]
            hi = [kv_block(b, slice(base + LANES, base + 2 * LANES)) for b in (j - 1, j, j + 1)]
            return jnp.concatenate(lo + hi, axis=0)

        kd = diag(g * 4 * LANES)
        vd = jnp.concatenate([diag(g * 4 * LANES + 2 * LANES), head_ones], axis=1)
        qa = att_ref[rows, KVD_WIDTH + g * 2 * LANES:KVD_WIDTH + g * 2 * LANES + LANES]
        qb = att_ref[rows, KVD_WIDTH + g * 2 * LANES + LANES:KVD_WIDTH + (g + 1) * 2 * LANES]
        qq = jnp.concatenate([qa, qb], axis=0)
        s = lax.dot_general(qq, kd, (((1,), (1,)), ((), ())), preferred_element_type=F32)
        probs = []
        sinks = []
        for pair in range(2):
            pr = slice(pair * WINDOW, (pair + 1) * WINDOW)
            row_p = []
            row_sink = []
            for half in range(2):
                off = half * 3 * WINDOW
                sink = sink_ref[4 * g + 2 * pair + half] * LOG2E
                s0 = s[pr, off:off + WINDOW] + lo_bias
                s1 = s[pr, off + WINDOW:off + 2 * WINDOW]
                s2 = s[pr, off + 2 * WINDOW:off + 3 * WINDOW] + hi_bias
                m = jnp.max(jnp.maximum(jnp.maximum(s0, s1), s2), axis=-1, keepdims=True)
                m = jnp.maximum(m, sink)
                row_p += [jnp.exp2(s0 - m).astype(BF16), jnp.exp2(s1 - m).astype(BF16),
                          jnp.exp2(s2 - m).astype(BF16)]
                row_sink.append(jnp.exp2(sink - m))
            probs.append(jnp.concatenate(row_p, axis=1))
            sinks.append(jnp.where(low_half, row_sink[0], row_sink[1]))
        pp = jnp.concatenate(probs, axis=0)
        o = jnp.dot(pp, vd, preferred_element_type=F32)
        for pair in range(2):
            pr = slice(pair * WINDOW, (pair + 1) * WINDOW)
            denom = o[pr, LANES:] + sinks[pair]
            attn_w[rows, (2 * g + pair) * LANES:(2 * g + pair + 1) * LANES] = o[pr, :LANES] / denom

    def step(cur, old, attend=True, project=True):
        n_out = D_MODEL // OUT_CHUNK
        for sub in range(t // SUB_TILE):
            rows = slice(sub * SUB_TILE, (sub + 1) * SUB_TILE)
            if project:
                mixed_ref[rows, 0:ATTN_WIDTH] = (_rms(attn_ref[old, rows, :], ga_ref[...])
                                                 * gate_ref[rows, 0:ATTN_WIDTH].astype(F32)).astype(BF16)
                mixed_ref[rows, ATTN_WIDTH:] = gate_ref[rows, ATTN_WIDTH:]
            qblocks = range(sub * SUB_TILE // WINDOW, (sub + 1) * SUB_TILE // WINDOW)
            units = [(j, g) for j in qblocks for g in range(N_KV_HEADS)]
            for u, (j, g) in enumerate(units):
                if project:
                    for c in range(u * n_out // len(units), (u + 1) * n_out // len(units)):
                        out_chunk(rows, c)
                if attend:
                    attn_unit(attn_ref.at[cur], j, g)
        if final and project:
            o_ref[...] = _rms(o_ref[...], fin_ref[...])

    @pl.when(s_id == 0)
    def _():
        step(0, 1, project=False)

    for parity in range(2):
        @pl.when((s_id % 2 == parity) & (s_id > 0) & (s_id < n))
        def _():
            step(parity, 1 - parity)

    @pl.when(s_id == n)
    def _():
        step(n % 2, 1 - n % 2, attend=False)


def _mixer(x, att_stream, gate_stream, sink, g_attn, w_out, final_norm, layer, seq, *, final):
    n_tok = x.shape[0]
    t = MIX_TILE
    n = n_tok // t
    tps = seq // t
    nblk = t // WINDOW
    att = lambda s: jnp.minimum(s, n - 1)
    gat = lambda s: jnp.maximum(s - 1, 0)
    tok = lambda stage, width: pl.BlockSpec((t, width), lambda s, *_: (stage(s), 0))
    grid_spec = pltpu.PrefetchScalarGridSpec(
        num_scalar_prefetch=1,
        grid=(n + 1,),
        in_specs=[
            tok(gat, D_MODEL),
            pl.BlockSpec((WINDOW, KVD_WIDTH), lambda s, *_: (jnp.maximum(att(s) * nblk - 1, 0), 0)),
            tok(att, ATT_WIDTH),
            pl.BlockSpec((WINDOW, KVD_WIDTH),
                         lambda s, *_: (jnp.minimum((att(s) + 1) * nblk, n_tok // WINDOW - 1), 0)),
            tok(gat, GATE_WIDTH),
            pl.BlockSpec((None, 1, ATTN_WIDTH), lambda s, *_: (layer, 0, 0)),
            pl.BlockSpec((None, D_MODEL, W_OUT_PITCH), lambda s, *_: (layer, 0, 0),
                         pipeline_mode=pl.Buffered(1)),
            pl.BlockSpec((1, D_MODEL), lambda s, *_: (0, 0)),
        ],
        out_specs=tok(gat, D_MODEL),
        scratch_shapes=[pltpu.VMEM((2, t, ATTN_WIDTH), F32), pltpu.VMEM((t, D_MODEL), BF16)],
    )
    return pl.pallas_call(
        functools.partial(_mixer_kernel, final=final, tiles_per_seq=tps, n=n),
        grid_spec=grid_spec,
        out_shape=jax.ShapeDtypeStruct((n_tok, D_MODEL), F32),
        compiler_params=pltpu.CompilerParams(
            dimension_semantics=("arbitrary",), vmem_limit_bytes=VMEM_LIMIT_BYTES),
        name="mixer",
    )(sink, x, att_stream, att_stream, att_stream, gate_stream, g_attn, w_out, final_norm)


def _trunk(x, mkv, mem_base, cos, sin, norm_in, w_in, attn_sink, conv_w, g_attn, g_conv, g_mem, w_out, final_norm):
    depth = w_in.shape[0]
    b, seq, _ = x.shape
    x = x.reshape(b * seq, D_MODEL)
    for l in range(depth):
        att, gate = _in_proj(x, mkv, mem_base, norm_in, w_in, conv_w, g_conv, g_mem, l, cos, sin, seq)
        x = _mixer(x, att, gate, attn_sink[l], g_attn, w_out, final_norm, l, seq, final=(l == depth - 1))
    return x.reshape(b, seq, D_MODEL)


def kernel(x_prompt, x_sample, mem_prompt, mem_sample, norm_in, w_in, attn_sink, conv_w, norm_mem, w_mem_kv,
           g_attn, g_conv, g_mem, w_out, final_norm):
    depth = w_in.shape[0]
    cos, sin = _rope_tables(max(x_prompt.shape[1], x_sample.shape[1]))
    mkv = _mem_kv(mem_prompt, mem_sample, norm_mem.reshape(depth, 1, D_MODEL), w_mem_kv.astype(BF16))
    weights = (norm_in.reshape(depth, 1, D_MODEL), w_in.astype(BF16), attn_sink, conv_w,
               g_attn.reshape(depth, 1, ATTN_WIDTH), g_conv.reshape(depth, 1, CONV_WIDTH),
               g_mem.reshape(depth, 1, X_WIDTH),
               jnp.pad(w_out.astype(BF16), ((0, 0), (0, 0), (0, W_OUT_PITCH - D_MODEL))),
               final_norm.reshape(1, D_MODEL))
    y_prompt = _trunk(x_prompt, mkv, 0, cos, sin, *weights)
    y_sample = _trunk(x_sample, mkv, mem_prompt.shape[0], cos, sin, *weights)
    return (y_prompt, y_sample)
```

```python
import functools

import jax
import jax.numpy as jnp
from jax import lax
from jax.experimental import pallas as pl
from jax.experimental.pallas import tpu as pltpu

D_MODEL = 2048
HEAD_DIM = 64
N_Q_HEADS = 16
N_KV_HEADS = 4
ATTN_WIDTH = N_Q_HEADS * HEAD_DIM
KV_WIDTH = N_KV_HEADS * HEAD_DIM
WINDOW = 128
CONV_WIDTH = 512
CONV_K = 3
N_MEM = 256
N_X_HEADS = 4
X_WIDTH = 512
X_HEAD_DIM = X_WIDTH // N_X_HEADS
ROPE_THETA = 10000.0
EPS = 1e-6
D_IN = 5632
CHUNK = 512

LANES = 128
F32_ROWS = 8
VMEM_LIMIT_BYTES = 56 * 1024 * 1024
LOG2E = 1.4426950408889634
NEG = -1e30
KVD_WIDTH = N_KV_HEADS * 4 * LANES
CM_WIDTH = CONV_WIDTH + X_WIDTH
ATT_WIDTH = KVD_WIDTH + ATTN_WIDTH
GATE_WIDTH = ATTN_WIDTH + CM_WIDTH

ROPE_TILE = 2048
PROJ_TILE = 512
MIX_TILE = 512
FINAL_MIX_TILE = 256
SUB_TILE = 256
OUT_CHUNK = 256
W_OUT_PITCH = D_MODEL + LANES

BF16 = jnp.bfloat16
F32 = jnp.float32


def _rms(x, g):
    ms = jnp.mean(x * x, axis=-1, keepdims=True)
    return (x * lax.rsqrt(ms + EPS)) * g


def _silu(x):
    return x / (1.0 + jnp.exp(-x))


def _rope_table_kernel(inv_ref, cos_ref, sin_ref, base_cos, base_sin):
    i = pl.program_id(0)
    nblk = cos_ref.shape[0] // WINDOW
    inv = inv_ref[...]

    @pl.when(i == 0)
    def _():
        ang = lax.broadcasted_iota(jnp.int32, (WINDOW, LANES), 0).astype(F32) * inv
        base_cos[...] = jnp.cos(ang)
        base_sin[...] = jnp.sin(ang)

    first_row = (lax.broadcasted_iota(jnp.int32, (nblk, LANES), 0) + i * nblk) * WINDOW
    ang0 = first_row.astype(F32) * inv
    cos0 = jnp.cos(ang0)
    sin0 = jnp.sin(ang0)
    lane = lax.broadcasted_iota(jnp.int32, (WINDOW, LANES), 1)
    first_half = (lane % HEAD_DIM) < HEAD_DIM // 2
    cb = base_cos[...]
    sb = base_sin[...]
    for b in range(nblk):
        c0 = cos0[b:b + 1, :]
        s0 = sin0[b:b + 1, :]
        cos_ref[b * WINDOW:(b + 1) * WINDOW, :] = c0 * cb - s0 * sb
        sn = s0 * cb + c0 * sb
        sin_ref[b * WINDOW:(b + 1) * WINDOW, :] = jnp.where(first_half, -sn, sn)


def _rope_tables(seq):
    inv_freq = ROPE_THETA ** (-jnp.arange(0, HEAD_DIM, 2, dtype=jnp.float32) / HEAD_DIM)
    inv = jnp.tile(inv_freq, LANES // (HEAD_DIM // 2)).reshape(1, LANES)
    tile = min(ROPE_TILE, seq)
    return pl.pallas_call(
        _rope_table_kernel,
        grid=(seq // tile,),
        in_specs=[pl.BlockSpec((1, LANES), lambda i: (0, 0))],
        out_specs=[pl.BlockSpec((tile, LANES), lambda i: (i, 0))] * 2,
        out_shape=[jax.ShapeDtypeStruct((seq, LANES), F32)] * 2,
        scratch_shapes=[pltpu.VMEM((WINDOW, LANES), F32)] * 2,
        compiler_params=pltpu.CompilerParams(dimension_semantics=("arbitrary",)),
        name="rope_tables",
    )(inv)


def _mem_kv_kernel(memp_ref, mems_ref, g_ref, w_ref, o_ref, *, n_prompt):
    mem = jnp.where(pl.program_id(1) < n_prompt, memp_ref[0], mems_ref[0])
    h = _rms(mem, g_ref[...]).astype(BF16)
    o_ref[0] = jnp.dot(h, w_ref[...].astype(BF16), preferred_element_type=F32).astype(BF16)


def _mem_kv(mem_prompt, mem_sample, g, w):
    depth = w.shape[0]
    n_p, n_s = mem_prompt.shape[0], mem_sample.shape[0]
    return pl.pallas_call(
        functools.partial(_mem_kv_kernel, n_prompt=n_p),
        grid=(depth, n_p + n_s),
        in_specs=[pl.BlockSpec((1, N_MEM, D_MODEL), lambda l, i: (jnp.minimum(i, n_p - 1), 0, 0)),
                  pl.BlockSpec((1, N_MEM, D_MODEL), lambda l, i: (jnp.maximum(i - n_p, 0), 0, 0)),
                  pl.BlockSpec((None, 1, D_MODEL), lambda l, i: (l, 0, 0)),
                  pl.BlockSpec((None, D_MODEL, 2 * X_WIDTH), lambda l, i: (l, 0, 0))],
        out_specs=pl.BlockSpec((None, 1, N_MEM, 2 * X_WIDTH), lambda l, i: (l, i, 0, 0)),
        out_shape=jax.ShapeDtypeStruct((depth, n_p + n_s, N_MEM, 2 * X_WIDTH), BF16),
        name="mem_kv",
    )(mem_prompt, mem_sample, g, w)


def _swap_halves(x):
    lane = lax.broadcasted_iota(jnp.int32, x.shape, 1)
    first = (lane % HEAD_DIM) < HEAD_DIM // 2
    return jnp.where(first, pltpu.roll(x, LANES - HEAD_DIM // 2, 1), pltpu.roll(x, HEAD_DIM // 2, 1))


def _in_proj_kernel(x_ref, xp_ref, xn_ref, g_ref, w_ref, cos_ref, sin_ref, mkv_ref, convw_ref, gc_ref, gm_ref,
                    att_ref, gate_ref, *, tiles_per_seq):
    t = PROJ_TILE
    i = pl.program_id(0)
    first = i % tiles_per_seq == 0
    last = i % tiles_per_seq == tiles_per_seq - 1
    g = g_ref[...]
    h_top = _rms(x_ref[:t // 2, :], g).astype(BF16)
    h_bot = _rms(x_ref[t // 2:, :], g).astype(BF16)
    h = jnp.concatenate([h_top, h_bot], axis=0)
    halo = jnp.concatenate([xp_ref[...], xn_ref[...]], axis=0)
    h_ext = jnp.concatenate([h, _rms(halo, g).astype(BF16)], axis=0)
    cos = cos_ref[...]
    sin = sin_ref[...]
    low_half = lax.broadcasted_iota(jnp.int32, cos.shape, 1) < HEAD_DIM

    def proj(c, lhs=h):
        return jnp.dot(lhs, w_ref[:, c * CHUNK:(c + 1) * CHUNK], preferred_element_type=F32)

    def rope(xc):
        return xc * cos + _swap_halves(xc) * sin

    mq = jnp.concatenate([proj(9, h_top), proj(9, h_bot)], axis=0).astype(BF16)
    xo = []
    for hd in range(N_X_HEADS):
        mk = mkv_ref[:, hd * X_HEAD_DIM:(hd + 1) * X_HEAD_DIM]
        mv = mkv_ref[:, X_WIDTH + hd * X_HEAD_DIM:X_WIDTH + (hd + 1) * X_HEAD_DIM]
        sm = lax.dot_general(mq[:, hd * X_HEAD_DIM:(hd + 1) * X_HEAD_DIM], mk, (((1,), (1,)), ((), ())),
                             preferred_element_type=F32)
        sm = sm * (X_HEAD_DIM ** -0.5)
        m = jnp.max(sm, axis=-1, keepdims=True)
        p = jnp.exp(sm - m)
        l = jnp.sum(p, axis=-1, keepdims=True)
        xo.append(jnp.dot(p.astype(BF16), mv, preferred_element_type=F32) * (1.0 / l))
    xo = jnp.concatenate(xo, axis=1)
    gate_ref[:, ATTN_WIDTH + CONV_WIDTH:] = (_rms(xo, gm_ref[...]) * _silu(proj(10))).astype(BF16)

    zc = proj(6, h_ext) * proj(7, h_ext)
    z = zc[:t]
    z_before = jnp.where(first, 0.0, zc[t + F32_ROWS - 1:t + F32_ROWS])
    z_after = jnp.where(last, 0.0, zc[t + F32_ROWS:t + F32_ROWS + 1])
    trow = lax.broadcasted_iota(jnp.int32, z.shape, 0)
    z_prev = jnp.where(trow == 0, z_before, pltpu.roll(z, 1, 0))
    z_next = jnp.where(trow == t - 1, z_after, pltpu.roll(z, t - 1, 0))
    cw = convw_ref[...]
    conv = proj(5) * (cw[0:1] * z_prev + cw[1:2] * z + cw[2:3] * z_next)
    gate_ref[:, ATTN_WIDTH:ATTN_WIDTH + CONV_WIDTH] = (_rms(conv, gc_ref[...]) * _silu(proj(8))).astype(BF16)

    pc = proj(2)
    for s in range(CHUNK // LANES):
        xc = pc[:, s * LANES:(s + 1) * LANES]
        is_v = s >= KV_WIDTH // LANES
        c = s - KV_WIDTH // LANES if is_v else s
        if not is_v:
            xc = rope(xc)
        sw = pltpu.roll(xc, HEAD_DIM, 1)
        for head, (lo_src, hi_src) in ((2 * c, (xc, sw)), (2 * c + 1, (sw, xc))):
            base = head * 4 * LANES + (2 * LANES if is_v else 0)
            att_ref[:, base:base + LANES] = jnp.where(low_half, lo_src, 0.0).astype(BF16)
            att_ref[:, base + LANES:base + 2 * LANES] = jnp.where(low_half, 0.0, hi_src).astype(BF16)
    for c in range(2):
        pc = proj(c)
        for s in range(CHUNK // LANES):
            xc = pc[:, s * LANES:(s + 1) * LANES]
            att_ref[:, KVD_WIDTH + c * CHUNK + s * LANES:KVD_WIDTH + c * CHUNK + (s + 1) * LANES] = (
                rope(xc) * (HEAD_DIM ** -0.5 * LOG2E)).astype(BF16)
    for dst, c in enumerate((3, 4)):
        gate_ref[:, dst * CHUNK:(dst + 1) * CHUNK] = _silu(proj(c)).astype(BF16)


def _in_proj(x, mkv, mem_base, g, w, conv_w, g_conv, g_mem, layer, cos, sin, seq):
    n_tok = x.shape[0]
    t = PROJ_TILE
    tps = seq // t
    nh = t // F32_ROWS
    tok = lambda width: pl.BlockSpec((t, width), lambda i: (i, 0))
    per_layer = lambda r, c: pl.BlockSpec((None, r, c), lambda i: (layer, 0, 0))
    return pl.pallas_call(
        functools.partial(_in_proj_kernel, tiles_per_seq=tps),
        grid=(n_tok // t,),
        in_specs=[tok(D_MODEL),
                  pl.BlockSpec((F32_ROWS, D_MODEL), lambda i: (jnp.maximum(i * nh - 1, 0), 0)),
                  pl.BlockSpec((F32_ROWS, D_MODEL), lambda i: (jnp.minimum((i + 1) * nh, n_tok // F32_ROWS - 1), 0)),
                  per_layer(1, D_MODEL),
                  pl.BlockSpec((None, D_MODEL, D_IN), lambda i: (layer, 0, 0), pipeline_mode=pl.Buffered(1)),
                  pl.BlockSpec((t, LANES), lambda i: (i % tps, 0)),
                  pl.BlockSpec((t, LANES), lambda i: (i % tps, 0)),
                  pl.BlockSpec((None, None, N_MEM, 2 * X_WIDTH), lambda i: (layer, mem_base + i // tps, 0, 0)),
                  per_layer(CONV_K, CONV_WIDTH), per_layer(1, CONV_WIDTH), per_layer(1, X_WIDTH)],
        out_specs=[tok(ATT_WIDTH), tok(GATE_WIDTH)],
        out_shape=[jax.ShapeDtypeStruct((n_tok, ATT_WIDTH), BF16),
                   jax.ShapeDtypeStruct((n_tok, GATE_WIDTH), BF16)],
        compiler_params=pltpu.CompilerParams(
            dimension_semantics=("parallel",), vmem_limit_bytes=VMEM_LIMIT_BYTES),
        name="in_proj",
    )(x, x, x, g, w, cos, sin, mkv, conv_w, g_conv, g_mem)


def _mixer_kernel(sink_ref, x_ref, kvp_ref, att_ref, kvn_ref, gate_ref, ga_ref, wout_ref, fin_ref,
                  o_ref, attn_ref, mixed_ref, *, final, tiles_per_seq, n, t):
    nqb = t // WINDOW
    s_id = pl.program_id(0)
    tile_a = jnp.minimum(s_id, n - 1)
    first_a = tile_a % tiles_per_seq == 0
    last_a = tile_a % tiles_per_seq == tiles_per_seq - 1

    row = lax.broadcasted_iota(jnp.int32, (WINDOW, WINDOW), 0)
    col = lax.broadcasted_iota(jnp.int32, (WINDOW, WINDOW), 1)
    low_half = lax.broadcasted_iota(jnp.int32, (WINDOW, LANES), 1) < HEAD_DIM
    ones_lo = jnp.where(low_half, 1.0, 0.0).astype(BF16)
    ones_hi = jnp.where(low_half, 0.0, 1.0).astype(BF16)
    head_ones = jnp.concatenate([ones_lo] * 3 + [ones_hi] * 3, axis=0)

    def kv_block(blk, cols):
        if blk < 0:
            return kvp_ref[:, cols]
        if blk >= nqb:
            return kvn_ref[:, cols]
        return att_ref[blk * WINDOW:(blk + 1) * WINDOW, cols]

    def out_chunk(rows, c):
        cols = slice(c * OUT_CHUNK, (c + 1) * OUT_CHUNK)
        o_ref[rows, cols] = x_ref[rows, cols] + jnp.dot(mixed_ref[rows, :], wout_ref[:, cols],
                                                        preferred_element_type=F32)

    def attn_unit(attn_w, j, g):
        lo_bias = jnp.where(col >= row, 0.0, NEG)
        hi_bias = jnp.where(col <= row, 0.0, NEG)
        if j == 0:
            lo_bias = lo_bias + jnp.where(first_a, NEG, 0.0)
        if j == nqb - 1:
            hi_bias = hi_bias + jnp.where(last_a, NEG, 0.0)
        rows = slice(j * WINDOW, (j + 1) * WINDOW)

        def diag(base):
            lo = [kv_block(b, slice(base, base + LANES)) for b in (j - 1, j, j + 1)]
            hi = [kv_block(b, slice(base + LANES, base + 2 * LANES)) for b in (j - 1, j, j + 1)]
            return jnp.concatenate(lo + hi, axis=0)

        kd = diag(g * 4 * LANES)
        vd = jnp.concatenate([diag(g * 4 * LANES + 2 * LANES), head_ones], axis=1)
        qa = att_ref[rows, KVD_WIDTH + g * 2 * LANES:KVD_WIDTH + g * 2 * LANES + LANES]
        qb = att_ref[rows, KVD_WIDTH + g * 2 * LANES + LANES:KVD_WIDTH + (g + 1) * 2 * LANES]
        qq = jnp.concatenate([qa, qb], axis=0)
        s = lax.dot_general(qq, kd, (((1,), (1,)), ((), ())), preferred_element_type=F32)
        probs = []
        sinks = []
        for pair in range(2):
            pr = slice(pair * WINDOW, (pair + 1) * WINDOW)
            row_p = []
            row_sink = []
            for half in range(2):
                off = half * 3 * WINDOW
                sink = sink_ref[4 * g + 2 * pair + half] * LOG2E
                s0 = s[pr, off:off + WINDOW] + lo_bias
                s1 = s[pr, off + WINDOW:off + 2 * WINDOW]
                s2 = s[pr, off + 2 * WINDOW:off + 3 * WINDOW] + hi_bias
                m = jnp.max(jnp.maximum(jnp.maximum(s0, s1), s2), axis=-1, keepdims=True)
                m = jnp.maximum(m, sink)
                row_p += [jnp.exp2(s0 - m).astype(BF16), jnp.exp2(s1 - m).astype(BF16),
                          jnp.exp2(s2 - m).astype(BF16)]
                row_sink.append(jnp.exp2(sink - m))
            probs.append(jnp.concatenate(row_p, axis=1))
            sinks.append(jnp.where(low_half, row_sink[0], row_sink[1]))
        pp = jnp.concatenate(probs, axis=0)
        o = jnp.dot(pp, vd, preferred_element_type=F32)
        for pair in range(2):
            pr = slice(pair * WINDOW, (pair + 1) * WINDOW)
            denom = o[pr, LANES:] + sinks[pair]
            attn_w[rows, (2 * g + pair) * LANES:(2 * g + pair + 1) * LANES] = o[pr, :LANES] / denom

    def step(cur, old, attend=True, project=True):
        n_out = D_MODEL // OUT_CHUNK
        for sub in range(t // SUB_TILE):
            rows = slice(sub * SUB_TILE, (sub + 1) * SUB_TILE)
            if project:
                mixed_ref[rows, 0:ATTN_WIDTH] = (_rms(attn_ref[old, rows, :], ga_ref[...])
                                                 * gate_ref[rows, 0:ATTN_WIDTH].astype(F32)).astype(BF16)
                mixed_ref[rows, ATTN_WIDTH:] = gate_ref[rows, ATTN_WIDTH:]
            qblocks = range(sub * SUB_TILE // WINDOW, (sub + 1) * SUB_TILE // WINDOW)
            units = [(j, g) for j in qblocks for g in range(N_KV_HEADS)]
            for u, (j, g) in enumerate(units):
                if project:
                    for c in range(u * n_out // len(units), (u + 1) * n_out // len(units)):
                        out_chunk(rows, c)
                if attend:
                    attn_unit(attn_ref.at[cur], j, g)
            if final and project:
                o_ref[rows, :] = _rms(o_ref[rows, :], fin_ref[...])

    @pl.when(s_id == 0)
    def _():
        step(0, 1, project=False)

    for parity in range(2):
        @pl.when((s_id % 2 == parity) & (s_id > 0) & (s_id < n))
        def _():
            step(parity, 1 - parity)

    @pl.when(s_id == n)
    def _():
        step(n % 2, 1 - n % 2, attend=False)


def _mixer(x, att_stream, gate_stream, sink, g_attn, w_out, final_norm, layer, seq, *, final):
    n_tok = x.shape[0]
    t = FINAL_MIX_TILE if final else MIX_TILE
    n = n_tok // t
    tps = seq // t
    nblk = t // WINDOW
    att = lambda s: jnp.minimum(s, n - 1)
    gat = lambda s: jnp.maximum(s - 1, 0)
    tok = lambda stage, width: pl.BlockSpec((t, width), lambda s, *_: (stage(s), 0))
    grid_spec = pltpu.PrefetchScalarGridSpec(
        num_scalar_prefetch=1,
        grid=(n + 1,),
        in_specs=[
            tok(gat, D_MODEL),
            pl.BlockSpec((WINDOW, KVD_WIDTH), lambda s, *_: (jnp.maximum(att(s) * nblk - 1, 0), 0)),
            tok(att, ATT_WIDTH),
            pl.BlockSpec((WINDOW, KVD_WIDTH),
                         lambda s, *_: (jnp.minimum((att(s) + 1) * nblk, n_tok // WINDOW - 1), 0)),
            tok(gat, GATE_WIDTH),
            pl.BlockSpec((None, 1, ATTN_WIDTH), lambda s, *_: (layer, 0, 0)),
            pl.BlockSpec((None, D_MODEL, W_OUT_PITCH), lambda s, *_: (layer, 0, 0),
                         pipeline_mode=pl.Buffered(1)),
            pl.BlockSpec((1, D_MODEL), lambda s, *_: (0, 0)),
        ],
        out_specs=tok(gat, D_MODEL),
        scratch_shapes=[pltpu.VMEM((2, t, ATTN_WIDTH), F32), pltpu.VMEM((t, D_MODEL), BF16)],
    )
    return pl.pallas_call(
        functools.partial(_mixer_kernel, final=final, tiles_per_seq=tps, n=n, t=t),
        grid_spec=grid_spec,
        out_shape=jax.ShapeDtypeStruct((n_tok, D_MODEL), F32),
        compiler_params=pltpu.CompilerParams(
            dimension_semantics=("arbitrary",), vmem_limit_bytes=VMEM_LIMIT_BYTES),
        name="mixer",
    )(sink, x, att_stream, att_stream, att_stream, gate_stream, g_attn, w_out, final_norm)


def _trunk(x, mkv, mem_base, cos, sin, norm_in, w_in, attn_sink, conv_w, g_attn, g_conv, g_mem, w_out, final_norm):
    depth = w_in.shape[0]
    b, seq, _ = x.shape
    x = x.reshape(b * seq, D_MODEL)
    for l in range(depth):
        att, gate = _in_proj(x, mkv, mem_base, norm_in, w_in, conv_w, g_conv, g_mem, l, cos, sin, seq)
        x = _mixer(x, att, gate, attn_sink[l], g_attn, w_out, final_norm, l, seq, final=(l == depth - 1))
    return x.reshape(b, seq, D_MODEL)


def kernel(x_prompt, x_sample, mem_prompt, mem_sample, norm_in, w_in, attn_sink, conv_w, norm_mem, w_mem_kv,
           g_attn, g_conv, g_mem, w_out, final_norm):
    depth = w_in.shape[0]
    cos, sin = _rope_tables(max(x_prompt.shape[1], x_sample.shape[1]))
    mkv = _mem_kv(mem_prompt, mem_sample, norm_mem.reshape(depth, 1, D_MODEL), w_mem_kv)
    w_out_pitched = jnp.concatenate(
        [w_out.astype(BF16), jnp.zeros((depth, D_MODEL, W_OUT_PITCH - D_MODEL), BF16)], axis=-1)
    weights = (norm_in.reshape(depth, 1, D_MODEL), w_in.astype(BF16), attn_sink, conv_w,
               g_attn.reshape(depth, 1, ATTN_WIDTH), g_conv.reshape(depth, 1, CONV_WIDTH),
               g_mem.reshape(depth, 1, X_WIDTH), w_out_pitched, final_norm.reshape(1, D_MODEL))
    y_prompt = _trunk(x_prompt, mkv, 0, cos, sin, *weights)
    y_sample = _trunk(x_sample, mkv, mem_prompt.shape[0], cos, sin, *weights)
    return (y_prompt, y_sample)
```

```python
import functools

import jax
import jax.numpy as jnp
from jax import lax
from jax.experimental import pallas as pl
from jax.experimental.pallas import tpu as pltpu

D_MODEL = 2048
HEAD_DIM = 64
N_Q_HEADS = 16
N_KV_HEADS = 4
ATTN_WIDTH = N_Q_HEADS * HEAD_DIM
KV_WIDTH = N_KV_HEADS * HEAD_DIM
WINDOW = 128
CONV_WIDTH = 512
CONV_K = 3
N_MEM = 256
N_X_HEADS = 4
X_WIDTH = 512
X_HEAD_DIM = X_WIDTH // N_X_HEADS
ROPE_THETA = 10000.0
EPS = 1e-6
D_IN = 5632
CHUNK = 512

LANES = 128
F32_ROWS = 8
VMEM_LIMIT_BYTES = 56 * 1024 * 1024
LOG2E = 1.4426950408889634
NEG = -1e30
KVD_WIDTH = N_KV_HEADS * 4 * LANES
CM_WIDTH = CONV_WIDTH + X_WIDTH
ATT_WIDTH = KVD_WIDTH + ATTN_WIDTH
GATE_WIDTH = ATTN_WIDTH + CM_WIDTH

ROPE_TILE = 2048
PROJ_TILE = 512
MIX_TILE = 512
FINAL_MIX_TILE = 512
SUB_TILE = 256
OUT_CHUNK = 256
W_OUT_PITCH = D_MODEL + LANES

BF16 = jnp.bfloat16
F32 = jnp.float32


def _rms(x, g):
    ms = jnp.mean(x * x, axis=-1, keepdims=True)
    return (x * lax.rsqrt(ms + EPS)) * g


def _silu(x):
    return x / (1.0 + jnp.exp(-x))


def _rope_table_kernel(inv_ref, cos_ref, sin_ref, base_cos, base_sin):
    i = pl.program_id(0)
    nblk = cos_ref.shape[0] // WINDOW
    inv = inv_ref[...]

    @pl.when(i == 0)
    def _():
        ang = lax.broadcasted_iota(jnp.int32, (WINDOW, LANES), 0).astype(F32) * inv
        base_cos[...] = jnp.cos(ang)
        base_sin[...] = jnp.sin(ang)

    first_row = (lax.broadcasted_iota(jnp.int32, (nblk, LANES), 0) + i * nblk) * WINDOW
    ang0 = first_row.astype(F32) * inv
    cos0 = jnp.cos(ang0)
    sin0 = jnp.sin(ang0)
    lane = lax.broadcasted_iota(jnp.int32, (WINDOW, LANES), 1)
    first_half = (lane % HEAD_DIM) < HEAD_DIM // 2
    cb = base_cos[...]
    sb = base_sin[...]
    for b in range(nblk):
        c0 = cos0[b:b + 1, :]
        s0 = sin0[b:b + 1, :]
        cos_ref[b * WINDOW:(b + 1) * WINDOW, :] = c0 * cb - s0 * sb
        sn = s0 * cb + c0 * sb
        sin_ref[b * WINDOW:(b + 1) * WINDOW, :] = jnp.where(first_half, -sn, sn)


def _rope_tables(seq):
    inv_freq = ROPE_THETA ** (-jnp.arange(0, HEAD_DIM, 2, dtype=jnp.float32) / HEAD_DIM)
    inv = jnp.tile(inv_freq, LANES // (HEAD_DIM // 2)).reshape(1, LANES)
    tile = min(ROPE_TILE, seq)
    return pl.pallas_call(
        _rope_table_kernel,
        grid=(seq // tile,),
        in_specs=[pl.BlockSpec((1, LANES), lambda i: (0, 0))],
        out_specs=[pl.BlockSpec((tile, LANES), lambda i: (i, 0))] * 2,
        out_shape=[jax.ShapeDtypeStruct((seq, LANES), F32)] * 2,
        scratch_shapes=[pltpu.VMEM((WINDOW, LANES), F32)] * 2,
        compiler_params=pltpu.CompilerParams(dimension_semantics=("arbitrary",)),
        name="rope_tables",
    )(inv)


def _mem_kv_kernel(memp_ref, mems_ref, g_ref, w_ref, o_ref, *, n_prompt):
    mem = jnp.where(pl.program_id(1) < n_prompt, memp_ref[0], mems_ref[0])
    h = _rms(mem, g_ref[...]).astype(BF16)
    o_ref[0] = jnp.dot(h, w_ref[...].astype(BF16), preferred_element_type=F32).astype(BF16)


def _mem_kv(mem_prompt, mem_sample, g, w):
    depth = w.shape[0]
    n_p, n_s = mem_prompt.shape[0], mem_sample.shape[0]
    return pl.pallas_call(
        functools.partial(_mem_kv_kernel, n_prompt=n_p),
        grid=(depth, n_p + n_s),
        in_specs=[pl.BlockSpec((1, N_MEM, D_MODEL), lambda l, i: (jnp.minimum(i, n_p - 1), 0, 0)),
                  pl.BlockSpec((1, N_MEM, D_MODEL), lambda l, i: (jnp.maximum(i - n_p, 0), 0, 0)),
                  pl.BlockSpec((None, 1, D_MODEL), lambda l, i: (l, 0, 0)),
                  pl.BlockSpec((None, D_MODEL, 2 * X_WIDTH), lambda l, i: (l, 0, 0))],
        out_specs=pl.BlockSpec((None, 1, N_MEM, 2 * X_WIDTH), lambda l, i: (l, i, 0, 0)),
        out_shape=jax.ShapeDtypeStruct((depth, n_p + n_s, N_MEM, 2 * X_WIDTH), BF16),
        name="mem_kv",
    )(mem_prompt, mem_sample, g, w)


def _swap_halves(x):
    lane = lax.broadcasted_iota(jnp.int32, x.shape, 1)
    first = (lane % HEAD_DIM) < HEAD_DIM // 2
    return jnp.where(first, pltpu.roll(x, LANES - HEAD_DIM // 2, 1), pltpu.roll(x, HEAD_DIM // 2, 1))


def _in_proj_kernel(x_ref, xp_ref, xn_ref, g_ref, w_ref, cos_ref, sin_ref, mkv_ref, convw_ref, gc_ref, gm_ref,
                    att_ref, gate_ref, *, tiles_per_seq, layer):
    t = PROJ_TILE
    i = pl.program_id(0)
    first = i % tiles_per_seq == 0
    last = i % tiles_per_seq == tiles_per_seq - 1
    g = g_ref[layer:layer + 1, :]
    h_top = _rms(x_ref[:t // 2, :], g).astype(BF16)
    h_bot = _rms(x_ref[t // 2:, :], g).astype(BF16)
    h = jnp.concatenate([h_top, h_bot], axis=0)
    halo = jnp.concatenate([xp_ref[...], xn_ref[...]], axis=0)
    h_ext = jnp.concatenate([h, _rms(halo, g).astype(BF16)], axis=0)
    cos = cos_ref[...]
    sin = sin_ref[...]
    low_half = lax.broadcasted_iota(jnp.int32, cos.shape, 1) < HEAD_DIM

    def proj(c, lhs=h):
        return jnp.dot(lhs, w_ref[:, c * CHUNK:(c + 1) * CHUNK], preferred_element_type=F32)

    def rope(xc):
        return xc * cos + _swap_halves(xc) * sin

    mq = jnp.concatenate([proj(9, h_top), proj(9, h_bot)], axis=0).astype(BF16)
    xo = []
    for hd in range(N_X_HEADS):
        mk = mkv_ref[:, hd * X_HEAD_DIM:(hd + 1) * X_HEAD_DIM]
        mv = mkv_ref[:, X_WIDTH + hd * X_HEAD_DIM:X_WIDTH + (hd + 1) * X_HEAD_DIM]
        sm = lax.dot_general(mq[:, hd * X_HEAD_DIM:(hd + 1) * X_HEAD_DIM], mk, (((1,), (1,)), ((), ())),
                             preferred_element_type=F32)
        sm = sm * (X_HEAD_DIM ** -0.5)
        m = jnp.max(sm, axis=-1, keepdims=True)
        p = jnp.exp(sm - m)
        l = jnp.sum(p, axis=-1, keepdims=True)
        xo.append(jnp.dot(p.astype(BF16), mv, preferred_element_type=F32) * (1.0 / l))
    xo = jnp.concatenate(xo, axis=1)
    gate_ref[:, ATTN_WIDTH + CONV_WIDTH:] = (
        _rms(xo, gm_ref[layer:layer + 1, :]) * _silu(proj(10))).astype(BF16)

    zc = proj(6, h_ext) * proj(7, h_ext)
    z = zc[:t]
    z_before = jnp.where(first, 0.0, zc[t + F32_ROWS - 1:t + F32_ROWS])
    z_after = jnp.where(last, 0.0, zc[t + F32_ROWS:t + F32_ROWS + 1])
    trow = lax.broadcasted_iota(jnp.int32, z.shape, 0)
    z_prev = jnp.where(trow == 0, z_before, pltpu.roll(z, 1, 0))
    z_next = jnp.where(trow == t - 1, z_after, pltpu.roll(z, t - 1, 0))
    cw = convw_ref[...]
    conv = proj(5) * (cw[0:1] * z_prev + cw[1:2] * z + cw[2:3] * z_next)
    gate_ref[:, ATTN_WIDTH:ATTN_WIDTH + CONV_WIDTH] = (
        _rms(conv, gc_ref[layer:layer + 1, :]) * _silu(proj(8))).astype(BF16)

    pc = proj(2)
    for s in range(CHUNK // LANES):
        xc = pc[:, s * LANES:(s + 1) * LANES]
        is_v = s >= KV_WIDTH // LANES
        c = s - KV_WIDTH // LANES if is_v else s
        if not is_v:
            xc = rope(xc)
        sw = pltpu.roll(xc, HEAD_DIM, 1)
        for head, (lo_src, hi_src) in ((2 * c, (xc, sw)), (2 * c + 1, (sw, xc))):
            base = head * 4 * LANES + (2 * LANES if is_v else 0)
            att_ref[:, base:base + LANES] = jnp.where(low_half, lo_src, 0.0).astype(BF16)
            att_ref[:, base + LANES:base + 2 * LANES] = jnp.where(low_half, 0.0, hi_src).astype(BF16)
    for c in range(2):
        pc = proj(c)
        for s in range(CHUNK // LANES):
            xc = pc[:, s * LANES:(s + 1) * LANES]
            att_ref[:, KVD_WIDTH + c * CHUNK + s * LANES:KVD_WIDTH + c * CHUNK + (s + 1) * LANES] = (
                rope(xc) * (HEAD_DIM ** -0.5 * LOG2E)).astype(BF16)
    for dst, c in enumerate((3, 4)):
        gate_ref[:, dst * CHUNK:(dst + 1) * CHUNK] = _silu(proj(c)).astype(BF16)


def _in_proj(x, mkv, mem_base, g, w, conv_w, g_conv, g_mem, layer, cos, sin, seq):
    n_tok = x.shape[0]
    t = PROJ_TILE
    tps = seq // t
    nh = t // F32_ROWS
    tok = lambda width: pl.BlockSpec((t, width), lambda i: (i, 0))
    whole = lambda a: pl.BlockSpec(a.shape, lambda i: (0,) * a.ndim)
    return pl.pallas_call(
        functools.partial(_in_proj_kernel, tiles_per_seq=tps, layer=layer),
        grid=(n_tok // t,),
        in_specs=[tok(D_MODEL),
                  pl.BlockSpec((F32_ROWS, D_MODEL), lambda i: (jnp.maximum(i * nh - 1, 0), 0)),
                  pl.BlockSpec((F32_ROWS, D_MODEL), lambda i: (jnp.minimum((i + 1) * nh, n_tok // F32_ROWS - 1), 0)),
                  whole(g),
                  pl.BlockSpec((None, D_MODEL, D_IN), lambda i: (layer, 0, 0), pipeline_mode=pl.Buffered(1)),
                  pl.BlockSpec((t, LANES), lambda i: (i % tps, 0)),
                  pl.BlockSpec((t, LANES), lambda i: (i % tps, 0)),
                  pl.BlockSpec((None, None, N_MEM, 2 * X_WIDTH), lambda i: (layer, mem_base + i // tps, 0, 0)),
                  pl.BlockSpec((None, CONV_K, CONV_WIDTH), lambda i: (layer, 0, 0)), whole(g_conv), whole(g_mem)],
        out_specs=[tok(ATT_WIDTH), tok(GATE_WIDTH)],
        out_shape=[jax.ShapeDtypeStruct((n_tok, ATT_WIDTH), BF16),
                   jax.ShapeDtypeStruct((n_tok, GATE_WIDTH), BF16)],
        compiler_params=pltpu.CompilerParams(
            dimension_semantics=("parallel",), vmem_limit_bytes=VMEM_LIMIT_BYTES),
        name="in_proj",
    )(x, x, x, g, w, cos, sin, mkv, conv_w, g_conv, g_mem)


def _mixer_kernel(sink_ref, x_ref, kvp_ref, att_ref, kvn_ref, gate_ref, ga_ref, wout_ref, fin_ref,
                  o_ref, attn_ref, mixed_ref, *, final, tiles_per_seq, n, t, layer):
    nqb = t // WINDOW
    s_id = pl.program_id(0)
    tile_a = jnp.minimum(s_id, n - 1)
    first_a = tile_a % tiles_per_seq == 0
    last_a = tile_a % tiles_per_seq == tiles_per_seq - 1

    row = lax.broadcasted_iota(jnp.int32, (WINDOW, WINDOW), 0)
    col = lax.broadcasted_iota(jnp.int32, (WINDOW, WINDOW), 1)
    low_half = lax.broadcasted_iota(jnp.int32, (WINDOW, LANES), 1) < HEAD_DIM
    ones_lo = jnp.where(low_half, 1.0, 0.0).astype(BF16)
    ones_hi = jnp.where(low_half, 0.0, 1.0).astype(BF16)
    head_ones = jnp.concatenate([ones_lo] * 3 + [ones_hi] * 3, axis=0)

    def kv_block(blk, cols):
        if blk < 0:
            return kvp_ref[:, cols]
        if blk >= nqb:
            return kvn_ref[:, cols]
        return att_ref[blk * WINDOW:(blk + 1) * WINDOW, cols]

    def out_chunk(rows, c):
        cols = slice(c * OUT_CHUNK, (c + 1) * OUT_CHUNK)
        o_ref[rows, cols] = x_ref[rows, cols] + jnp.dot(mixed_ref[rows, :], wout_ref[:, cols],
                                                        preferred_element_type=F32)

    def attn_unit(attn_w, j, g):
        lo_bias = jnp.where(col >= row, 0.0, NEG)
        hi_bias = jnp.where(col <= row, 0.0, NEG)
        if j == 0:
            lo_bias = lo_bias + jnp.where(first_a, NEG, 0.0)
        if j == nqb - 1:
            hi_bias = hi_bias + jnp.where(last_a, NEG, 0.0)
        rows = slice(j * WINDOW, (j + 1) * WINDOW)

        def diag(base):
            lo = [kv_block(b, slice(base, base + LANES)) for b in (j - 1, j, j + 1)]
            hi = [kv_block(b, slice(base + LANES, base + 2 * LANES)) for b in (j - 1, j, j + 1)]
            return jnp.concatenate(lo + hi, axis=0)

        kd = diag(g * 4 * LANES)
        vd = jnp.concatenate([diag(g * 4 * LANES + 2 * LANES), head_ones], axis=1)
        qa = att_ref[rows, KVD_WIDTH + g * 2 * LANES:KVD_WIDTH + g * 2 * LANES + LANES]
        qb = att_ref[rows, KVD_WIDTH + g * 2 * LANES + LANES:KVD_WIDTH + (g + 1) * 2 * LANES]
        qq = jnp.concatenate([qa, qb], axis=0)
        s = lax.dot_general(qq, kd, (((1,), (1,)), ((), ())), preferred_element_type=F32)
        probs = []
        sinks = []
        for pair in range(2):
            pr = slice(pair * WINDOW, (pair + 1) * WINDOW)
            row_p = []
            row_sink = []
            for half in range(2):
                off = half * 3 * WINDOW
                sink = sink_ref[4 * g + 2 * pair + half] * LOG2E
                s0 = s[pr, off:off + WINDOW] + lo_bias
                s1 = s[pr, off + WINDOW:off + 2 * WINDOW]
                s2 = s[pr, off + 2 * WINDOW:off + 3 * WINDOW] + hi_bias
                m = jnp.max(jnp.maximum(jnp.maximum(s0, s1), s2), axis=-1, keepdims=True)
                m = jnp.maximum(m, sink)
                row_p += [jnp.exp2(s0 - m).astype(BF16), jnp.exp2(s1 - m).astype(BF16),
                          jnp.exp2(s2 - m).astype(BF16)]
                row_sink.append(jnp.exp2(sink - m))
            probs.append(jnp.concatenate(row_p, axis=1))
            sinks.append(jnp.where(low_half, row_sink[0], row_sink[1]))
        pp = jnp.concatenate(probs, axis=0)
        o = jnp.dot(pp, vd, preferred_element_type=F32)
        for pair in range(2):
            pr = slice(pair * WINDOW, (pair + 1) * WINDOW)
            denom = o[pr, LANES:] + sinks[pair]
            attn_w[rows, (2 * g + pair) * LANES:(2 * g + pair + 1) * LANES] = o[pr, :LANES] / denom

    def step(cur, old, attend=True, project=True):
        n_out = D_MODEL // OUT_CHUNK
        for sub in range(t // SUB_TILE):
            rows = slice(sub * SUB_TILE, (sub + 1) * SUB_TILE)
            if project:
                mixed_ref[rows, 0:ATTN_WIDTH] = (_rms(attn_ref[old, rows, :], ga_ref[layer:layer + 1, :])
                                                 * gate_ref[rows, 0:ATTN_WIDTH].astype(F32)).astype(BF16)
                mixed_ref[rows, ATTN_WIDTH:] = gate_ref[rows, ATTN_WIDTH:]
            qblocks = range(sub * SUB_TILE // WINDOW, (sub + 1) * SUB_TILE // WINDOW)
            units = [(j, g) for j in qblocks for g in range(N_KV_HEADS)]
            for u, (j, g) in enumerate(units):
                if project:
                    for c in range(u * n_out // len(units), (u + 1) * n_out // len(units)):
                        out_chunk(rows, c)
                if attend:
                    attn_unit(attn_ref.at[cur], j, g)
            if final and project:
                o_ref[rows, :] = _rms(o_ref[rows, :], fin_ref[...])

    @pl.when(s_id == 0)
    def _():
        step(0, 1, project=False)

    for parity in range(2):
        @pl.when((s_id % 2 == parity) & (s_id > 0) & (s_id < n))
        def _():
            step(parity, 1 - parity)

    @pl.when(s_id == n)
    def _():
        step(n % 2, 1 - n % 2, attend=False)


def _mixer(x, att_stream, gate_stream, sink, g_attn, w_out, final_norm, layer, seq, *, final):
    n_tok = x.shape[0]
    t = FINAL_MIX_TILE if final else MIX_TILE
    n = n_tok // t
    tps = seq // t
    nblk = t // WINDOW
    att = lambda s: jnp.minimum(s, n - 1)
    gat = lambda s: jnp.maximum(s - 1, 0)
    tok = lambda stage, width: pl.BlockSpec((t, width), lambda s, *_: (stage(s), 0))
    grid_spec = pltpu.PrefetchScalarGridSpec(
        num_scalar_prefetch=1,
        grid=(n + 1,),
        in_specs=[
            tok(gat, D_MODEL),
            pl.BlockSpec((WINDOW, KVD_WIDTH), lambda s, *_: (jnp.maximum(att(s) * nblk - 1, 0), 0)),
            tok(att, ATT_WIDTH),
            pl.BlockSpec((WINDOW, KVD_WIDTH),
                         lambda s, *_: (jnp.minimum((att(s) + 1) * nblk, n_tok // WINDOW - 1), 0)),
            tok(gat, GATE_WIDTH),
            pl.BlockSpec(g_attn.shape, lambda s, *_: (0, 0)),
            pl.BlockSpec((None, D_MODEL, W_OUT_PITCH), lambda s, *_: (layer, 0, 0),
                         pipeline_mode=pl.Buffered(1)),
            pl.BlockSpec((1, D_MODEL), lambda s, *_: (0, 0)),
        ],
        out_specs=tok(gat, D_MODEL),
        scratch_shapes=[pltpu.VMEM((2, t, ATTN_WIDTH), F32), pltpu.VMEM((t, D_MODEL), BF16)],
    )
    return pl.pallas_call(
        functools.partial(_mixer_kernel, final=final, tiles_per_seq=tps, n=n, t=t, layer=layer),
        grid_spec=grid_spec,
        out_shape=jax.ShapeDtypeStruct((n_tok, D_MODEL), F32),
        compiler_params=pltpu.CompilerParams(
            dimension_semantics=("arbitrary",), vmem_limit_bytes=VMEM_LIMIT_BYTES),
        name="mixer",
    )(sink, x, att_stream, att_stream, att_stream, gate_stream, g_attn, w_out, final_norm)


def _trunk(x, mkv, mem_base, cos, sin, norm_in, w_in, attn_sink, conv_w, g_attn, g_conv, g_mem, w_out, final_norm):
    depth = w_in.shape[0]
    b, seq, _ = x.shape
    x = x.reshape(b * seq, D_MODEL)
    for l in range(depth):
        att, gate = _in_proj(x, mkv, mem_base, norm_in, w_in, conv_w, g_conv, g_mem, l, cos, sin, seq)
        x = _mixer(x, att, gate, attn_sink[l], g_attn, w_out, final_norm, l, seq, final=(l == depth - 1))
    return x.reshape(b, seq, D_MODEL)


def kernel(x_prompt, x_sample, mem_prompt, mem_sample, norm_in, w_in, attn_sink, conv_w, norm_mem, w_mem_kv,
           g_attn, g_conv, g_mem, w_out, final_norm):
    depth = w_in.shape[0]
    cos, sin = _rope_tables(max(x_prompt.shape[1], x_sample.shape[1]))
    mkv = _mem_kv(mem_prompt, mem_sample, norm_mem.reshape(depth, 1, D_MODEL), w_mem_kv)
    w_out_pitched = jnp.concatenate(
        [w_out.astype(BF16), jnp.zeros((depth, D_MODEL, W_OUT_PITCH - D_MODEL), BF16)], axis=-1)
    weights = (norm_in, w_in.astype(BF16), attn_sink, conv_w, g_attn, g_conv, g_mem, w_out_pitched,
               final_norm.reshape(1, D_MODEL))
    y_prompt = _trunk(x_prompt, mkv, 0, cos, sin, *weights)
    y_sample = _trunk(x_sample, mkv, mem_prompt.shape[0], cos, sin, *weights)
    return (y_prompt, y_sample)
```

```python
import functools

import jax
import jax.numpy as jnp
from jax import lax
from jax.experimental import pallas as pl
from jax.experimental.pallas import tpu as pltpu

D_MODEL = 2048
HEAD_DIM = 64
N_Q_HEADS = 16
N_KV_HEADS = 4
ATTN_WIDTH = N_Q_HEADS * HEAD_DIM
KV_WIDTH = N_KV_HEADS * HEAD_DIM
WINDOW = 128
CONV_WIDTH = 512
CONV_K = 3
N_MEM = 256
N_X_HEADS = 4
X_WIDTH = 512
X_HEAD_DIM = X_WIDTH // N_X_HEADS
ROPE_THETA = 10000.0
EPS = 1e-6
D_IN = 5632
CHUNK = 512

LANES = 128
F32_ROWS = 8
VMEM_LIMIT_BYTES = 56 * 1024 * 1024
LOG2E = 1.4426950408889634
NEG = -1e30
KVD_WIDTH = N_KV_HEADS * 4 * LANES
CM_WIDTH = CONV_WIDTH + X_WIDTH
ATT_WIDTH = KVD_WIDTH + ATTN_WIDTH
GATE_WIDTH = ATTN_WIDTH + CM_WIDTH

ROPE_TILE = 2048
PROJ_TILE = 512
MIX_TILE = 256
SUB_TILE = 256
OUT_CHUNK = 256
W_OUT_PITCH = D_MODEL + LANES

BF16 = jnp.bfloat16
F32 = jnp.float32


def _rms(x, g):
    ms = jnp.mean(x * x, axis=-1, keepdims=True)
    return (x * lax.rsqrt(ms + EPS)) * g


def _silu(x):
    return x / (1.0 + jnp.exp(-x))


def _rope_table_kernel(inv_ref, cos_ref, sin_ref, base_cos, base_sin):
    i = pl.program_id(0)
    nblk = cos_ref.shape[0] // WINDOW
    inv = inv_ref[...]

    @pl.when(i == 0)
    def _():
        ang = lax.broadcasted_iota(jnp.int32, (WINDOW, LANES), 0).astype(F32) * inv
        base_cos[...] = jnp.cos(ang)
        base_sin[...] = jnp.sin(ang)

    first_row = (lax.broadcasted_iota(jnp.int32, (nblk, LANES), 0) + i * nblk) * WINDOW
    ang0 = first_row.astype(F32) * inv
    cos0 = jnp.cos(ang0)
    sin0 = jnp.sin(ang0)
    lane = lax.broadcasted_iota(jnp.int32, (WINDOW, LANES), 1)
    first_half = (lane % HEAD_DIM) < HEAD_DIM // 2
    cb = base_cos[...]
    sb = base_sin[...]
    for b in range(nblk):
        c0 = cos0[b:b + 1, :]
        s0 = sin0[b:b + 1, :]
        cos_ref[b * WINDOW:(b + 1) * WINDOW, :] = c0 * cb - s0 * sb
        sn = s0 * cb + c0 * sb
        sin_ref[b * WINDOW:(b + 1) * WINDOW, :] = jnp.where(first_half, -sn, sn)


def _rope_tables(seq):
    inv_freq = ROPE_THETA ** (-jnp.arange(0, HEAD_DIM, 2, dtype=jnp.float32) / HEAD_DIM)
    inv = jnp.tile(inv_freq, LANES // (HEAD_DIM // 2)).reshape(1, LANES)
    tile = min(ROPE_TILE, seq)
    return pl.pallas_call(
        _rope_table_kernel,
        grid=(seq // tile,),
        in_specs=[pl.BlockSpec((1, LANES), lambda i: (0, 0))],
        out_specs=[pl.BlockSpec((tile, LANES), lambda i: (i, 0))] * 2,
        out_shape=[jax.ShapeDtypeStruct((seq, LANES), F32)] * 2,
        scratch_shapes=[pltpu.VMEM((WINDOW, LANES), F32)] * 2,
        compiler_params=pltpu.CompilerParams(dimension_semantics=("arbitrary",)),
        name="rope_tables",
    )(inv)


def _mem_kv_kernel(memp_ref, mems_ref, g_ref, w_ref, o_ref, *, n_prompt):
    mem = jnp.where(pl.program_id(1) < n_prompt, memp_ref[0], mems_ref[0])
    h = _rms(mem, g_ref[...]).astype(BF16)
    o_ref[0] = jnp.dot(h, w_ref[...].astype(BF16), preferred_element_type=F32).astype(BF16)


def _mem_kv(mem_prompt, mem_sample, g, w):
    depth = w.shape[0]
    n_p, n_s = mem_prompt.shape[0], mem_sample.shape[0]
    return pl.pallas_call(
        functools.partial(_mem_kv_kernel, n_prompt=n_p),
        grid=(depth, n_p + n_s),
        in_specs=[pl.BlockSpec((1, N_MEM, D_MODEL), lambda l, i: (jnp.minimum(i, n_p - 1), 0, 0)),
                  pl.BlockSpec((1, N_MEM, D_MODEL), lambda l, i: (jnp.maximum(i - n_p, 0), 0, 0)),
                  pl.BlockSpec((None, 1, D_MODEL), lambda l, i: (l, 0, 0)),
                  pl.BlockSpec((None, D_MODEL, 2 * X_WIDTH), lambda l, i: (l, 0, 0))],
        out_specs=pl.BlockSpec((None, 1, N_MEM, 2 * X_WIDTH), lambda l, i: (l, i, 0, 0)),
        out_shape=jax.ShapeDtypeStruct((depth, n_p + n_s, N_MEM, 2 * X_WIDTH), BF16),
        name="mem_kv",
    )(mem_prompt, mem_sample, g, w)


def _swap_halves(x):
    lane = lax.broadcasted_iota(jnp.int32, x.shape, 1)
    first = (lane % HEAD_DIM) < HEAD_DIM // 2
    return jnp.where(first, pltpu.roll(x, LANES - HEAD_DIM // 2, 1), pltpu.roll(x, HEAD_DIM // 2, 1))


def _in_proj_kernel(x_ref, xp_ref, xn_ref, g_ref, w_ref, cos_ref, sin_ref, mkv_ref, convw_ref, gc_ref, gm_ref,
                    att_ref, gate_ref, *, tiles_per_seq, layer):
    t = PROJ_TILE
    i = pl.program_id(0)
    first = i % tiles_per_seq == 0
    last = i % tiles_per_seq == tiles_per_seq - 1
    g = g_ref[layer:layer + 1, :]
    h_top = _rms(x_ref[:t // 2, :], g).astype(BF16)
    h_bot = _rms(x_ref[t // 2:, :], g).astype(BF16)
    h = jnp.concatenate([h_top, h_bot], axis=0)
    halo = jnp.concatenate([xp_ref[...], xn_ref[...]], axis=0)
    h_ext = jnp.concatenate([h, _rms(halo, g).astype(BF16)], axis=0)
    cos = cos_ref[...]
    sin = sin_ref[...]
    low_half = lax.broadcasted_iota(jnp.int32, cos.shape, 1) < HEAD_DIM

    def proj(c, lhs=h):
        return jnp.dot(lhs, w_ref[:, c * CHUNK:(c + 1) * CHUNK], preferred_element_type=F32)

    def rope(xc):
        return xc * cos + _swap_halves(xc) * sin

    mq = jnp.concatenate([proj(9, h_top), proj(9, h_bot)], axis=0).astype(BF16)
    xo = []
    for hd in range(N_X_HEADS):
        mk = mkv_ref[:, hd * X_HEAD_DIM:(hd + 1) * X_HEAD_DIM]
        mv = mkv_ref[:, X_WIDTH + hd * X_HEAD_DIM:X_WIDTH + (hd + 1) * X_HEAD_DIM]
        sm = lax.dot_general(mq[:, hd * X_HEAD_DIM:(hd + 1) * X_HEAD_DIM], mk, (((1,), (1,)), ((), ())),
                             preferred_element_type=F32)
        sm = sm * (X_HEAD_DIM ** -0.5)
        m = jnp.max(sm, axis=-1, keepdims=True)
        p = jnp.exp(sm - m)
        l = jnp.sum(p, axis=-1, keepdims=True)
        xo.append(jnp.dot(p.astype(BF16), mv, preferred_element_type=F32) * (1.0 / l))
    xo = jnp.concatenate(xo, axis=1)
    gate_ref[:, ATTN_WIDTH + CONV_WIDTH:] = (
        _rms(xo, gm_ref[layer:layer + 1, :]) * _silu(proj(10))).astype(BF16)

    zc = proj(6, h_ext) * proj(7, h_ext)
    z = zc[:t]
    z_before = jnp.where(first, 0.0, zc[t + F32_ROWS - 1:t + F32_ROWS])
    z_after = jnp.where(last, 0.0, zc[t + F32_ROWS:t + F32_ROWS + 1])
    trow = lax.broadcasted_iota(jnp.int32, z.shape, 0)
    z_prev = jnp.where(trow == 0, z_before, pltpu.roll(z, 1, 0))
    z_next = jnp.where(trow == t - 1, z_after, pltpu.roll(z, t - 1, 0))
    cw = convw_ref[...]
    conv = proj(5) * (cw[0:1] * z_prev + cw[1:2] * z + cw[2:3] * z_next)
    gate_ref[:, ATTN_WIDTH:ATTN_WIDTH + CONV_WIDTH] = (
        _rms(conv, gc_ref[layer:layer + 1, :]) * _silu(proj(8))).astype(BF16)

    pc = proj(2)
    for s in range(CHUNK // LANES):
        xc = pc[:, s * LANES:(s + 1) * LANES]
        is_v = s >= KV_WIDTH // LANES
        c = s - KV_WIDTH // LANES if is_v else s
        if not is_v:
            xc = rope(xc)
        sw = pltpu.roll(xc, HEAD_DIM, 1)
        for head, (lo_src, hi_src) in ((2 * c, (xc, sw)), (2 * c + 1, (sw, xc))):
            base = head * 4 * LANES + (2 * LANES if is_v else 0)
            att_ref[:, base:base + LANES] = jnp.where(low_half, lo_src, 0.0).astype(BF16)
            att_ref[:, base + LANES:base + 2 * LANES] = jnp.where(low_half, 0.0, hi_src).astype(BF16)
    for c in range(2):
        pc = proj(c)
        for s in range(CHUNK // LANES):
            xc = pc[:, s * LANES:(s + 1) * LANES]
            att_ref[:, KVD_WIDTH + c * CHUNK + s * LANES:KVD_WIDTH + c * CHUNK + (s + 1) * LANES] = (
                rope(xc) * (HEAD_DIM ** -0.5 * LOG2E)).astype(BF16)
    for dst, c in enumerate((3, 4)):
        gate_ref[:, dst * CHUNK:(dst + 1) * CHUNK] = _silu(proj(c)).astype(BF16)


def _in_proj(x, mkv, mem_base, g, w, conv_w, g_conv, g_mem, layer, cos, sin, seq):
    n_tok = x.shape[0]
    t = PROJ_TILE
    tps = seq // t
    nh = t // F32_ROWS
    tok = lambda width: pl.BlockSpec((t, width), lambda i: (i, 0))
    whole = lambda a: pl.BlockSpec(a.shape, lambda i: (0,) * a.ndim)
    return pl.pallas_call(
        functools.partial(_in_proj_kernel, tiles_per_seq=tps, layer=layer),
        grid=(n_tok // t,),
        in_specs=[tok(D_MODEL),
                  pl.BlockSpec((F32_ROWS, D_MODEL), lambda i: (jnp.maximum(i * nh - 1, 0), 0)),
                  pl.BlockSpec((F32_ROWS, D_MODEL), lambda i: (jnp.minimum((i + 1) * nh, n_tok // F32_ROWS - 1), 0)),
                  whole(g),
                  pl.BlockSpec((None, D_MODEL, D_IN), lambda i: (layer, 0, 0), pipeline_mode=pl.Buffered(1)),
                  pl.BlockSpec((t, LANES), lambda i: (i % tps, 0)),
                  pl.BlockSpec((t, LANES), lambda i: (i % tps, 0)),
                  pl.BlockSpec((None, None, N_MEM, 2 * X_WIDTH), lambda i: (layer, mem_base + i // tps, 0, 0)),
                  pl.BlockSpec((None, CONV_K, CONV_WIDTH), lambda i: (layer, 0, 0)), whole(g_conv), whole(g_mem)],
        out_specs=[tok(ATT_WIDTH), tok(GATE_WIDTH)],
        out_shape=[jax.ShapeDtypeStruct((n_tok, ATT_WIDTH), BF16),
                   jax.ShapeDtypeStruct((n_tok, GATE_WIDTH), BF16)],
        compiler_params=pltpu.CompilerParams(
            dimension_semantics=("parallel",), vmem_limit_bytes=VMEM_LIMIT_BYTES),
        name="in_proj",
    )(x, x, x, g, w, cos, sin, mkv, conv_w, g_conv, g_mem)


def _mixer_kernel(sink_ref, x_ref, kvp_ref, att_ref, kvn_ref, gate_ref, ga_ref, wout_ref, fin_ref,
                  o_ref, attn_ref, mixed_ref, *, final, tiles_per_seq, n, t):
    nqb = t // WINDOW
    s_id = pl.program_id(0)
    tile_a = jnp.minimum(s_id, n - 1)
    first_a = tile_a % tiles_per_seq == 0
    last_a = tile_a % tiles_per_seq == tiles_per_seq - 1

    row = lax.broadcasted_iota(jnp.int32, (WINDOW, WINDOW), 0)
    col = lax.broadcasted_iota(jnp.int32, (WINDOW, WINDOW), 1)
    low_half = lax.broadcasted_iota(jnp.int32, (WINDOW, LANES), 1) < HEAD_DIM
    ones_lo = jnp.where(low_half, 1.0, 0.0).astype(BF16)
    ones_hi = jnp.where(low_half, 0.0, 1.0).astype(BF16)
    head_ones = jnp.concatenate([ones_lo] * 3 + [ones_hi] * 3, axis=0)

    def kv_block(blk, cols):
        if blk < 0:
            return kvp_ref[:, cols]
        if blk >= nqb:
            return kvn_ref[:, cols]
        return att_ref[blk * WINDOW:(blk + 1) * WINDOW, cols]

    def out_chunk(rows, c):
        cols = slice(c * OUT_CHUNK, (c + 1) * OUT_CHUNK)
        o_ref[rows, cols] = x_ref[rows, cols] + jnp.dot(mixed_ref[rows, :], wout_ref[:, cols],
                                                        preferred_element_type=F32)

    def attn_unit(attn_w, j, g):
        lo_bias = jnp.where(col >= row, 0.0, NEG)
        hi_bias = jnp.where(col <= row, 0.0, NEG)
        if j == 0:
            lo_bias = lo_bias + jnp.where(first_a, NEG, 0.0)
        if j == nqb - 1:
            hi_bias = hi_bias + jnp.where(last_a, NEG, 0.0)
        rows = slice(j * WINDOW, (j + 1) * WINDOW)

        def diag(base):
            lo = [kv_block(b, slice(base, base + LANES)) for b in (j - 1, j, j + 1)]
            hi = [kv_block(b, slice(base + LANES, base + 2 * LANES)) for b in (j - 1, j, j + 1)]
            return jnp.concatenate(lo + hi, axis=0)

        kd = diag(g * 4 * LANES)
        vd = jnp.concatenate([diag(g * 4 * LANES + 2 * LANES), head_ones], axis=1)
        qa = att_ref[rows, KVD_WIDTH + g * 2 * LANES:KVD_WIDTH + g * 2 * LANES + LANES]
        qb = att_ref[rows, KVD_WIDTH + g * 2 * LANES + LANES:KVD_WIDTH + (g + 1) * 2 * LANES]
        qq = jnp.concatenate([qa, qb], axis=0)
        s = lax.dot_general(qq, kd, (((1,), (1,)), ((), ())), preferred_element_type=F32)
        probs = []
        sinks = []
        for pair in range(2):
            pr = slice(pair * WINDOW, (pair + 1) * WINDOW)
            row_p = []
            row_sink = []
            for half in range(2):
                off = half * 3 * WINDOW
                sink = sink_ref[4 * g + 2 * pair + half] * LOG2E
                s0 = s[pr, off:off + WINDOW] + lo_bias
                s1 = s[pr, off + WINDOW:off + 2 * WINDOW]
                s2 = s[pr, off + 2 * WINDOW:off + 3 * WINDOW] + hi_bias
                m = jnp.max(jnp.maximum(jnp.maximum(s0, s1), s2), axis=-1, keepdims=True)
                m = jnp.maximum(m, sink)
                row_p += [jnp.exp2(s0 - m).astype(BF16), jnp.exp2(s1 - m).astype(BF16),
                          jnp.exp2(s2 - m).astype(BF16)]
                row_sink.append(jnp.exp2(sink - m))
            probs.append(jnp.concatenate(row_p, axis=1))
            sinks.append(jnp.where(low_half, row_sink[0], row_sink[1]))
        pp = jnp.concatenate(probs, axis=0)
        o = jnp.dot(pp, vd, preferred_element_type=F32)
        for pair in range(2):
            pr = slice(pair * WINDOW, (pair + 1) * WINDOW)
            denom = o[pr, LANES:] + sinks[pair]
            attn_w[rows, (2 * g + pair) * LANES:(2 * g + pair + 1) * LANES] = o[pr, :LANES] / denom

    def step(cur, old, attend=True, project=True):
        n_out = D_MODEL // OUT_CHUNK
        for sub in range(t // SUB_TILE):
            rows = slice(sub * SUB_TILE, (sub + 1) * SUB_TILE)
            if project:
                mixed_ref[rows, 0:ATTN_WIDTH] = (_rms(attn_ref[old, rows, :], ga_ref[...])
                                                 * gate_ref[rows, 0:ATTN_WIDTH].astype(F32)).astype(BF16)
                mixed_ref[rows, ATTN_WIDTH:] = gate_ref[rows, ATTN_WIDTH:]
            qblocks = range(sub * SUB_TILE // WINDOW, (sub + 1) * SUB_TILE // WINDOW)
            units = [(j, g) for j in qblocks for g in range(N_KV_HEADS)]
            for u, (j, g) in enumerate(units):
                if project:
                    for c in range(u * n_out // len(units), (u + 1) * n_out // len(units)):
                        out_chunk(rows, c)
                if attend:
                    attn_unit(attn_ref.at[cur], j, g)
            if final and project:
                o_ref[rows, :] = _rms(o_ref[rows, :], fin_ref[...])

    @pl.when(s_id == 0)
    def _():
        step(0, 1, project=False)

    for parity in range(2):
        @pl.when((s_id % 2 == parity) & (s_id > 0) & (s_id < n))
        def _():
            step(parity, 1 - parity)

    @pl.when(s_id == n)
    def _():
        step(n % 2, 1 - n % 2, attend=False)


def _mixer(x, att_stream, gate_stream, sink, g_attn, w_out, final_norm, layer, seq, *, final):
    n_tok = x.shape[0]
    t = MIX_TILE
    n = n_tok // t
    tps = seq // t
    nblk = t // WINDOW
    att = lambda s: jnp.minimum(s, n - 1)
    gat = lambda s: jnp.maximum(s - 1, 0)
    tok = lambda stage, width: pl.BlockSpec((t, width), lambda s, *_: (stage(s), 0))
    grid_spec = pltpu.PrefetchScalarGridSpec(
        num_scalar_prefetch=1,
        grid=(n + 1,),
        in_specs=[
            tok(gat, D_MODEL),
            pl.BlockSpec((WINDOW, KVD_WIDTH), lambda s, *_: (jnp.maximum(att(s) * nblk - 1, 0), 0)),
            tok(att, ATT_WIDTH),
            pl.BlockSpec((WINDOW, KVD_WIDTH),
                         lambda s, *_: (jnp.minimum((att(s) + 1) * nblk, n_tok // WINDOW - 1), 0)),
            tok(gat, GATE_WIDTH),
            pl.BlockSpec((None, 1, ATTN_WIDTH), lambda s, *_: (layer, 0, 0)),
            pl.BlockSpec((None, D_MODEL, W_OUT_PITCH), lambda s, *_: (layer, 0, 0),
                         pipeline_mode=pl.Buffered(1)),
            pl.BlockSpec((1, D_MODEL), lambda s, *_: (0, 0)),
        ],
        out_specs=tok(gat, D_MODEL),
        scratch_shapes=[pltpu.VMEM((2, t, ATTN_WIDTH), F32), pltpu.VMEM((t, D_MODEL), BF16)],
    )
    return pl.pallas_call(
        functools.partial(_mixer_kernel, final=final, tiles_per_seq=tps, n=n, t=t),
        grid_spec=grid_spec,
        out_shape=jax.ShapeDtypeStruct((n_tok, D_MODEL), F32),
        compiler_params=pltpu.CompilerParams(
            dimension_semantics=("arbitrary",), vmem_limit_bytes=VMEM_LIMIT_BYTES),
        name="mixer",
    )(sink, x, att_stream, att_stream, att_stream, gate_stream, g_attn, w_out, final_norm)


def _trunk(x, mkv, mem_base, cos, sin, norm_in, w_in, attn_sink, conv_w, g_attn, g_conv, g_mem, w_out, final_norm):
    depth = w_in.shape[0]
    b, seq, _ = x.shape
    x = x.reshape(b * seq, D_MODEL)
    for l in range(depth):
        att, gate = _in_proj(x, mkv, mem_base, norm_in, w_in, conv_w, g_conv, g_mem, l, cos, sin, seq)
        x = _mixer(x, att, gate, attn_sink[l], g_attn, w_out, final_norm, l, seq, final=(l == depth - 1))
    return x.reshape(b, seq, D_MODEL)


def kernel(x_prompt, x_sample, mem_prompt, mem_sample, norm_in, w_in, attn_sink, conv_w, norm_mem, w_mem_kv,
           g_attn, g_conv, g_mem, w_out, final_norm):
    depth = w_in.shape[0]
    cos, sin = _rope_tables(max(x_prompt.shape[1], x_sample.shape[1]))
    mkv = _mem_kv(mem_prompt, mem_sample, norm_mem.reshape(depth, 1, D_MODEL), w_mem_kv)
    w_out_pitched = jnp.concatenate(
        [w_out.astype(BF16), jnp.zeros((depth, D_MODEL, W_OUT_PITCH - D_MODEL), BF16)], axis=-1)
    weights = (norm_in, w_in.astype(BF16), attn_sink, conv_w, g_attn.reshape(depth, 1, ATTN_WIDTH), g_conv, g_mem,
               w_out_pitched, final_norm.reshape(1, D_MODEL))
    y_prompt = _trunk(x_prompt, mkv, 0, cos, sin, *weights)
    y_sample = _trunk(x_sample, mkv, mem_prompt.shape[0], cos, sin, *weights)
    return (y_prompt, y_sample)
```

```python
import functools

import jax
import jax.numpy as jnp
from jax import lax
from jax.experimental import pallas as pl
from jax.experimental.pallas import tpu as pltpu

D_MODEL = 2048
HEAD_DIM = 64
N_Q_HEADS = 16
N_KV_HEADS = 4
ATTN_WIDTH = N_Q_HEADS * HEAD_DIM
KV_WIDTH = N_KV_HEADS * HEAD_DIM
WINDOW = 128
CONV_WIDTH = 512
CONV_K = 3
N_MEM = 256
N_X_HEADS = 4
X_WIDTH = 512
X_HEAD_DIM = X_WIDTH // N_X_HEADS
ROPE_THETA = 10000.0
EPS = 1e-6
D_IN = 5632
CHUNK = 512

LANES = 128
F32_ROWS = 8
VMEM_LIMIT_BYTES = 56 * 1024 * 1024
LOG2E = 1.4426950408889634
NEG = -1e30
KVD_WIDTH = N_KV_HEADS * 4 * LANES
CM_WIDTH = CONV_WIDTH + X_WIDTH
ATT_WIDTH = KVD_WIDTH + ATTN_WIDTH
GATE_WIDTH = ATTN_WIDTH + CM_WIDTH

ROPE_TILE = 2048
PROJ_TILE = 512
MIX_TILE = 256
SUB_TILE = 256
OUT_CHUNK = 256
W_OUT_PITCH = D_MODEL + LANES

BF16 = jnp.bfloat16
F32 = jnp.float32


def _rms(x, g):
    ms = jnp.mean(x * x, axis=-1, keepdims=True)
    return (x * lax.rsqrt(ms + EPS)) * g


def _silu(x):
    return x / (1.0 + jnp.exp(-x))


def _rope_table_kernel(inv_ref, cos_ref, sin_ref, base_cos, base_sin):
    i = pl.program_id(0)
    nblk = cos_ref.shape[0] // WINDOW
    inv = inv_ref[...]

    @pl.when(i == 0)
    def _():
        ang = lax.broadcasted_iota(jnp.int32, (WINDOW, LANES), 0).astype(F32) * inv
        base_cos[...] = jnp.cos(ang)
        base_sin[...] = jnp.sin(ang)

    first_row = (lax.broadcasted_iota(jnp.int32, (nblk, LANES), 0) + i * nblk) * WINDOW
    ang0 = first_row.astype(F32) * inv
    cos0 = jnp.cos(ang0)
    sin0 = jnp.sin(ang0)
    lane = lax.broadcasted_iota(jnp.int32, (WINDOW, LANES), 1)
    first_half = (lane % HEAD_DIM) < HEAD_DIM // 2
    cb = base_cos[...]
    sb = base_sin[...]
    for b in range(nblk):
        c0 = cos0[b:b + 1, :]
        s0 = sin0[b:b + 1, :]
        cos_ref[b * WINDOW:(b + 1) * WINDOW, :] = c0 * cb - s0 * sb
        sn = s0 * cb + c0 * sb
        sin_ref[b * WINDOW:(b + 1) * WINDOW, :] = jnp.where(first_half, -sn, sn)


def _rope_tables(seq):
    inv_freq = ROPE_THETA ** (-jnp.arange(0, HEAD_DIM, 2, dtype=jnp.float32) / HEAD_DIM)
    inv = jnp.tile(inv_freq, LANES // (HEAD_DIM // 2)).reshape(1, LANES)
    tile = min(ROPE_TILE, seq)
    return pl.pallas_call(
        _rope_table_kernel,
        grid=(seq // tile,),
        in_specs=[pl.BlockSpec((1, LANES), lambda i: (0, 0))],
        out_specs=[pl.BlockSpec((tile, LANES), lambda i: (i, 0))] * 2,
        out_shape=[jax.ShapeDtypeStruct((seq, LANES), F32)] * 2,
        scratch_shapes=[pltpu.VMEM((WINDOW, LANES), F32)] * 2,
        compiler_params=pltpu.CompilerParams(dimension_semantics=("arbitrary",)),
        name="rope_tables",
    )(inv)


def _mem_kv_kernel(memp_ref, mems_ref, g_ref, w_ref, o_ref, *, n_prompt):
    mem = jnp.where(pl.program_id(1) < n_prompt, memp_ref[0], mems_ref[0])
    h = _rms(mem, g_ref[...]).astype(BF16)
    o_ref[0] = jnp.dot(h, w_ref[...].astype(BF16), preferred_element_type=F32).astype(BF16)


def _mem_kv(mem_prompt, mem_sample, g, w):
    depth = w.shape[0]
    n_p, n_s = mem_prompt.shape[0], mem_sample.shape[0]
    return pl.pallas_call(
        functools.partial(_mem_kv_kernel, n_prompt=n_p),
        grid=(depth, n_p + n_s),
        in_specs=[pl.BlockSpec((1, N_MEM, D_MODEL), lambda l, i: (jnp.minimum(i, n_p - 1), 0, 0)),
                  pl.BlockSpec((1, N_MEM, D_MODEL), lambda l, i: (jnp.maximum(i - n_p, 0), 0, 0)),
                  pl.BlockSpec((None, 1, D_MODEL), lambda l, i: (l, 0, 0)),
                  pl.BlockSpec((None, D_MODEL, 2 * X_WIDTH), lambda l, i: (l, 0, 0))],
        out_specs=pl.BlockSpec((None, 1, N_MEM, 2 * X_WIDTH), lambda l, i: (l, i, 0, 0)),
        out_shape=jax.ShapeDtypeStruct((depth, n_p + n_s, N_MEM, 2 * X_WIDTH), BF16),
        name="mem_kv",
    )(mem_prompt, mem_sample, g, w)


def _swap_halves(x):
    lane = lax.broadcasted_iota(jnp.int32, x.shape, 1)
    first = (lane % HEAD_DIM) < HEAD_DIM // 2
    return jnp.where(first, pltpu.roll(x, LANES - HEAD_DIM // 2, 1), pltpu.roll(x, HEAD_DIM // 2, 1))


def _in_proj_kernel(x_ref, xp_ref, xn_ref, g_ref, w_ref, cos_ref, sin_ref, mkv_ref, convw_ref, gc_ref, gm_ref,
                    att_ref, gate_ref, *, tiles_per_seq, layer):
    t = PROJ_TILE
    i = pl.program_id(0)
    first = i % tiles_per_seq == 0
    last = i % tiles_per_seq == tiles_per_seq - 1
    g = g_ref[layer:layer + 1, :]
    h_top = _rms(x_ref[:t // 2, :], g).astype(BF16)
    h_bot = _rms(x_ref[t // 2:, :], g).astype(BF16)
    h = jnp.concatenate([h_top, h_bot], axis=0)
    halo = jnp.concatenate([xp_ref[...], xn_ref[...]], axis=0)
    h_ext = jnp.concatenate([h, _rms(halo, g).astype(BF16)], axis=0)
    cos = cos_ref[...]
    sin = sin_ref[...]
    low_half = lax.broadcasted_iota(jnp.int32, cos.shape, 1) < HEAD_DIM

    def proj(c, lhs=h):
        return jnp.dot(lhs, w_ref[:, c * CHUNK:(c + 1) * CHUNK], preferred_element_type=F32)

    def rope(xc):
        return xc * cos + _swap_halves(xc) * sin

    mq = jnp.concatenate([proj(9, h_top), proj(9, h_bot)], axis=0).astype(BF16)
    xo = []
    for hd in range(N_X_HEADS):
        mk = mkv_ref[:, hd * X_HEAD_DIM:(hd + 1) * X_HEAD_DIM]
        mv = mkv_ref[:, X_WIDTH + hd * X_HEAD_DIM:X_WIDTH + (hd + 1) * X_HEAD_DIM]
        sm = lax.dot_general(mq[:, hd * X_HEAD_DIM:(hd + 1) * X_HEAD_DIM], mk, (((1,), (1,)), ((), ())),
                             preferred_element_type=F32)
        sm = sm * (X_HEAD_DIM ** -0.5)
        m = jnp.max(sm, axis=-1, keepdims=True)
        p = jnp.exp(sm - m)
        l = jnp.sum(p, axis=-1, keepdims=True)
        xo.append(jnp.dot(p.astype(BF16), mv, preferred_element_type=F32) * (1.0 / l))
    xo = jnp.concatenate(xo, axis=1)
    gate_ref[:, ATTN_WIDTH + CONV_WIDTH:] = (
        _rms(xo, gm_ref[layer:layer + 1, :]) * _silu(proj(10))).astype(BF16)

    zc = proj(6, h_ext) * proj(7, h_ext)
    z = zc[:t]
    z_before = jnp.where(first, 0.0, zc[t + F32_ROWS - 1:t + F32_ROWS])
    z_after = jnp.where(last, 0.0, zc[t + F32_ROWS:t + F32_ROWS + 1])
    trow = lax.broadcasted_iota(jnp.int32, z.shape, 0)
    z_prev = jnp.where(trow == 0, z_before, pltpu.roll(z, 1, 0))
    z_next = jnp.where(trow == t - 1, z_after, pltpu.roll(z, t - 1, 0))
    cw = convw_ref[...]
    conv = proj(5) * (cw[0:1] * z_prev + cw[1:2] * z + cw[2:3] * z_next)
    gate_ref[:, ATTN_WIDTH:ATTN_WIDTH + CONV_WIDTH] = (
        _rms(conv, gc_ref[layer:layer + 1, :]) * _silu(proj(8))).astype(BF16)

    pc = proj(2)
    for s in range(CHUNK // LANES):
        xc = pc[:, s * LANES:(s + 1) * LANES]
        is_v = s >= KV_WIDTH // LANES
        c = s - KV_WIDTH // LANES if is_v else s
        if not is_v:
            xc = rope(xc)
        sw = pltpu.roll(xc, HEAD_DIM, 1)
        for head, (lo_src, hi_src) in ((2 * c, (xc, sw)), (2 * c + 1, (sw, xc))):
            base = head * 4 * LANES + (2 * LANES if is_v else 0)
            att_ref[:, base:base + LANES] = jnp.where(low_half, lo_src, 0.0).astype(BF16)
            att_ref[:, base + LANES:base + 2 * LANES] = jnp.where(low_half, 0.0, hi_src).astype(BF16)
    for c in range(2):
        pc = proj(c)
        for s in range(CHUNK // LANES):
            xc = pc[:, s * LANES:(s + 1) * LANES]
            att_ref[:, KVD_WIDTH + c * CHUNK + s * LANES:KVD_WIDTH + c * CHUNK + (s + 1) * LANES] = (
                rope(xc) * (HEAD_DIM ** -0.5 * LOG2E)).astype(BF16)
    for dst, c in enumerate((3, 4)):
        gate_ref[:, dst * CHUNK:(dst + 1) * CHUNK] = _silu(proj(c)).astype(BF16)


def _in_proj(x, mkv, mem_base, g, w, conv_w, g_conv, g_mem, layer, cos, sin, seq):
    n_tok = x.shape[0]
    t = PROJ_TILE
    tps = seq // t
    nh = t // F32_ROWS
    tok = lambda width: pl.BlockSpec((t, width), lambda i: (i, 0))
    whole = lambda a: pl.BlockSpec(a.shape, lambda i: (0,) * a.ndim)
    return pl.pallas_call(
        functools.partial(_in_proj_kernel, tiles_per_seq=tps, layer=layer),
        grid=(n_tok // t,),
        in_specs=[tok(D_MODEL),
                  pl.BlockSpec((F32_ROWS, D_MODEL), lambda i: (jnp.maximum(i * nh - 1, 0), 0)),
                  pl.BlockSpec((F32_ROWS, D_MODEL), lambda i: (jnp.minimum((i + 1) * nh, n_tok // F32_ROWS - 1), 0)),
                  whole(g),
                  pl.BlockSpec((None, D_MODEL, D_IN), lambda i: (layer, 0, 0), pipeline_mode=pl.Buffered(1)),
                  pl.BlockSpec((t, LANES), lambda i: (i % tps, 0)),
                  pl.BlockSpec((t, LANES), lambda i: (i % tps, 0)),
                  pl.BlockSpec((None, None, N_MEM, 2 * X_WIDTH), lambda i: (layer, mem_base + i // tps, 0, 0)),
                  pl.BlockSpec((None, CONV_K, CONV_WIDTH), lambda i: (layer, 0, 0)), whole(g_conv), whole(g_mem)],
        out_specs=[tok(ATT_WIDTH), tok(GATE_WIDTH)],
        out_shape=[jax.ShapeDtypeStruct((n_tok, ATT_WIDTH), BF16),
                   jax.ShapeDtypeStruct((n_tok, GATE_WIDTH), BF16)],
        compiler_params=pltpu.CompilerParams(
            dimension_semantics=("parallel",), vmem_limit_bytes=VMEM_LIMIT_BYTES),
        name="in_proj",
    )(x, x, x, g, w, cos, sin, mkv, conv_w, g_conv, g_mem)


def _mixer_kernel(sink_ref, x_ref, kvp_ref, att_ref, kvn_ref, gate_ref, ga_ref, wout_ref, fin_ref,
                  o_ref, attn_ref, mixed_ref, *, final, tiles_per_seq, n, t):
    nqb = t // WINDOW
    s_id = pl.program_id(0)
    tile_a = jnp.minimum(s_id, n - 1)
    first_a = tile_a % tiles_per_seq == 0
    last_a = tile_a % tiles_per_seq == tiles_per_seq - 1

    row = lax.broadcasted_iota(jnp.int32, (WINDOW, WINDOW), 0)
    col = lax.broadcasted_iota(jnp.int32, (WINDOW, WINDOW), 1)
    low_half = lax.broadcasted_iota(jnp.int32, (WINDOW, LANES), 1) < HEAD_DIM
    ones_lo = jnp.where(low_half, 1.0, 0.0).astype(BF16)
    ones_hi = jnp.where(low_half, 0.0, 1.0).astype(BF16)
    head_ones = jnp.concatenate([ones_lo] * 3 + [ones_hi] * 3, axis=0)

    def kv_block(blk, cols):
        if blk < 0:
            return kvp_ref[:, cols]
        if blk >= nqb:
            return kvn_ref[:, cols]
        return att_ref[blk * WINDOW:(blk + 1) * WINDOW, cols]

    def out_chunk(rows, c):
        cols = slice(c * OUT_CHUNK, (c + 1) * OUT_CHUNK)
        y = x_ref[rows, cols] + jnp.dot(mixed_ref[rows, :], wout_ref[:, cols], preferred_element_type=F32)
        o_ref[rows, cols] = y
        return jnp.sum(y * y, axis=-1, keepdims=True) if final else None

    def attn_unit(attn_w, j, g):
        lo_bias = jnp.where(col >= row, 0.0, NEG)
        hi_bias = jnp.where(col <= row, 0.0, NEG)
        if j == 0:
            lo_bias = lo_bias + jnp.where(first_a, NEG, 0.0)
        if j == nqb - 1:
            hi_bias = hi_bias + jnp.where(last_a, NEG, 0.0)
        rows = slice(j * WINDOW, (j + 1) * WINDOW)

        def diag(base):
            lo = [kv_block(b, slice(base, base + LANES)) for b in (j - 1, j, j + 1)]
            hi = [kv_block(b, slice(base + LANES, base + 2 * LANES)) for b in (j - 1, j, j + 1)]
            return jnp.concatenate(lo + hi, axis=0)

        kd = diag(g * 4 * LANES)
        vd = jnp.concatenate([diag(g * 4 * LANES + 2 * LANES), head_ones], axis=1)
        qa = att_ref[rows, KVD_WIDTH + g * 2 * LANES:KVD_WIDTH + g * 2 * LANES + LANES]
        qb = att_ref[rows, KVD_WIDTH + g * 2 * LANES + LANES:KVD_WIDTH + (g + 1) * 2 * LANES]
        qq = jnp.concatenate([qa, qb], axis=0)
        s = lax.dot_general(qq, kd, (((1,), (1,)), ((), ())), preferred_element_type=F32)
        probs = []
        sinks = []
        for pair in range(2):
            pr = slice(pair * WINDOW, (pair + 1) * WINDOW)
            row_p = []
            row_sink = []
            for half in range(2):
                off = half * 3 * WINDOW
                sink = sink_ref[4 * g + 2 * pair + half] * LOG2E
                s0 = s[pr, off:off + WINDOW] + lo_bias
                s1 = s[pr, off + WINDOW:off + 2 * WINDOW]
                s2 = s[pr, off + 2 * WINDOW:off + 3 * WINDOW] + hi_bias
                m = jnp.max(jnp.maximum(jnp.maximum(s0, s1), s2), axis=-1, keepdims=True)
                m = jnp.maximum(m, sink)
                row_p += [jnp.exp2(s0 - m).astype(BF16), jnp.exp2(s1 - m).astype(BF16),
                          jnp.exp2(s2 - m).astype(BF16)]
                row_sink.append(jnp.exp2(sink - m))
            probs.append(jnp.concatenate(row_p, axis=1))
            sinks.append(jnp.where(low_half, row_sink[0], row_sink[1]))
        pp = jnp.concatenate(probs, axis=0)
        o = jnp.dot(pp, vd, preferred_element_type=F32)
        for pair in range(2):
            pr = slice(pair * WINDOW, (pair + 1) * WINDOW)
            denom = o[pr, LANES:] + sinks[pair]
            attn_w[rows, (2 * g + pair) * LANES:(2 * g + pair + 1) * LANES] = o[pr, :LANES] / denom

    def step(cur, old, attend=True, project=True):
        n_out = D_MODEL // OUT_CHUNK
        for sub in range(t // SUB_TILE):
            rows = slice(sub * SUB_TILE, (sub + 1) * SUB_TILE)
            if project:
                mixed_ref[rows, 0:ATTN_WIDTH] = (_rms(attn_ref[old, rows, :], ga_ref[...])
                                                 * gate_ref[rows, 0:ATTN_WIDTH].astype(F32)).astype(BF16)
                mixed_ref[rows, ATTN_WIDTH:] = gate_ref[rows, ATTN_WIDTH:]
            qblocks = range(sub * SUB_TILE // WINDOW, (sub + 1) * SUB_TILE // WINDOW)
            units = [(j, g) for j in qblocks for g in range(N_KV_HEADS)]
            sq = []
            for u, (j, g) in enumerate(units):
                if project:
                    for c in range(u * n_out // len(units), (u + 1) * n_out // len(units)):
                        sq.append(out_chunk(rows, c))
                if attend:
                    attn_unit(attn_ref.at[cur], j, g)
            if final and project:
                scale = lax.rsqrt(sum(sq) * (1.0 / D_MODEL) + EPS)
                o_ref[rows, :] = (o_ref[rows, :] * scale) * fin_ref[...]

    @pl.when(s_id == 0)
    def _():
        step(0, 1, project=False)

    for parity in range(2):
        @pl.when((s_id % 2 == parity) & (s_id > 0) & (s_id < n))
        def _():
            step(parity, 1 - parity)

    @pl.when(s_id == n)
    def _():
        step(n % 2, 1 - n % 2, attend=False)


def _mixer(x, att_stream, gate_stream, sink, g_attn, w_out, final_norm, layer, seq, *, final):
    n_tok = x.shape[0]
    t = MIX_TILE
    n = n_tok // t
    tps = seq // t
    nblk = t // WINDOW
    att = lambda s: jnp.minimum(s, n - 1)
    gat = lambda s: jnp.maximum(s - 1, 0)
    tok = lambda stage, width: pl.BlockSpec((t, width), lambda s, *_: (stage(s), 0))
    grid_spec = pltpu.PrefetchScalarGridSpec(
        num_scalar_prefetch=1,
        grid=(n + 1,),
        in_specs=[
            tok(gat, D_MODEL),
            pl.BlockSpec((WINDOW, KVD_WIDTH), lambda s, *_: (jnp.maximum(att(s) * nblk - 1, 0), 0)),
            tok(att, ATT_WIDTH),
            pl.BlockSpec((WINDOW, KVD_WIDTH),
                         lambda s, *_: (jnp.minimum((att(s) + 1) * nblk, n_tok // WINDOW - 1), 0)),
            tok(gat, GATE_WIDTH),
            pl.BlockSpec((None, 1, ATTN_WIDTH), lambda s, *_: (layer, 0, 0)),
            pl.BlockSpec((None, D_MODEL, W_OUT_PITCH), lambda s, *_: (layer, 0, 0),
                         pipeline_mode=pl.Buffered(1)),
            pl.BlockSpec((1, D_MODEL), lambda s, *_: (0, 0)),
        ],
        out_specs=tok(gat, D_MODEL),
        scratch_shapes=[pltpu.VMEM((2, t, ATTN_WIDTH), F32), pltpu.VMEM((t, D_MODEL), BF16)],
    )
    return pl.pallas_call(
        functools.partial(_mixer_kernel, final=final, tiles_per_seq=tps, n=n, t=t),
        grid_spec=grid_spec,
        out_shape=jax.ShapeDtypeStruct((n_tok, D_MODEL), F32),
        compiler_params=pltpu.CompilerParams(
            dimension_semantics=("arbitrary",), vmem_limit_bytes=VMEM_LIMIT_BYTES),
        name="mixer",
    )(sink, x, att_stream, att_stream, att_stream, gate_stream, g_attn, w_out, final_norm)


def _trunk(x, mkv, mem_base, cos, sin, norm_in, w_in, attn_sink, conv_w, g_attn, g_conv, g_mem, w_out, final_norm):
    depth = w_in.shape[0]
    b, seq, _ = x.shape
    x = x.reshape(b * seq, D_MODEL)
    for l in range(depth):
        att, gate = _in_proj(x, mkv, mem_base, norm_in, w_in, conv_w, g_conv, g_mem, l, cos, sin, seq)
        x = _mixer(x, att, gate, attn_sink[l], g_attn, w_out, final_norm, l, seq, final=(l == depth - 1))
    return x.reshape(b, seq, D_MODEL)


def kernel(x_prompt, x_sample, mem_prompt, mem_sample, norm_in, w_in, attn_sink, conv_w, norm_mem, w_mem_kv,
           g_attn, g_conv, g_mem, w_out, final_norm):
    depth = w_in.shape[0]
    cos, sin = _rope_tables(max(x_prompt.shape[1], x_sample.shape[1]))
    mkv = _mem_kv(mem_prompt, mem_sample, norm_mem.reshape(depth, 1, D_MODEL), w_mem_kv)
    w_out_pitched = jnp.concatenate(
        [w_out.astype(BF16), jnp.zeros((depth, D_MODEL, W_OUT_PITCH - D_MODEL), BF16)], axis=-1)
    weights = (norm_in, w_in.astype(BF16), attn_sink, conv_w, g_attn.reshape(depth, 1, ATTN_WIDTH), g_conv, g_mem,
               w_out_pitched, final_norm.reshape(1, D_MODEL))
    y_prompt = _trunk(x_prompt, mkv, 0, cos, sin, *weights)
    y_sample = _trunk(x_sample, mkv, mem_prompt.shape[0], cos, sin, *weights)
    return (y_prompt, y_sample)
```
